```python
import jax, jax.numpy as jnp
from jax import lax
import numpy as np

D_MODEL = 1024
BATCH = 8
SEQ = 4096
DEPTH = 2
DEC_BATCH = 32
DEC_SEQ = 8
PAST_LEN = 16384
PAGE_SIZE = 128

HEAD_DIM = 64
GROUP_WIDTH = D_MODEL // 4
MIX_WIDTH = 4 * GROUP_WIDTH
ATT_HEADS = GROUP_WIDTH // HEAD_DIM
MOBA_BLOCK = 256
MOBA_TOPK = 3
MOBA_QBLOCK = 128
ROPE_THETA = 10000.0
CONF_TAPS = 31
SC_TAPS = 3
RWKV_HEADS = GROUP_WIDTH // HEAD_DIM
RWKV_DECAY_RANK = 32
RWKV_AAA_RANK = 32
RWKV_GATE_RANK = 64
RWKV_PROJ = 3 * GROUP_WIDTH + RWKV_DECAY_RANK + RWKV_AAA_RANK + RWKV_GATE_RANK
ATT_COLS = 3 * GROUP_WIDTH
CONF_COLS = 2 * GROUP_WIDTH
SC_COLS = 3 * GROUP_WIDTH
IN_COLS = ATT_COLS + CONF_COLS + RWKV_PROJ + SC_COLS
PEER_HEADS = 8
PEER_NKEYS = 128
PEER_EXPERTS = PEER_NKEYS * PEER_NKEYS
PEER_QDIM = 128
PEER_TOPK = 16
PEER_TOKBLOCK = 512
RMS_EPS = 1e-6
LN_EPS = 1e-5
RWKV_GN_EPS = 64e-5
NEG_INF = -1e30

kernel_name = 'hybrid_moba_conformer_rwkv7_shortconv_peer_step'


def rms_norm(x, g):
    xf = x.astype(jnp.float32)
    y = xf * lax.rsqrt(jnp.mean(xf * xf, -1, keepdims=True) + RMS_EPS)
    return (y * g.astype(jnp.float32)).astype(x.dtype)


def layer_norm(x, g, b, eps):
    xf = x.astype(jnp.float32)
    mu = jnp.mean(xf, -1, keepdims=True)
    var = jnp.mean(jnp.square(xf - mu), -1, keepdims=True)
    return ((xf - mu) * lax.rsqrt(var + eps) * g.astype(jnp.float32) + b.astype(jnp.float32)).astype(x.dtype)


def rope(x, pos):
    half = HEAD_DIM // 2
    inv_freq = ROPE_THETA ** (-jnp.arange(half, dtype=jnp.float32) / half)
    ang = pos.astype(jnp.float32)[:, None] * inv_freq[None, :]
    cos = jnp.cos(ang)[None, :, None, :]
    sin = jnp.sin(ang)[None, :, None, :]
    xf = x.astype(jnp.float32)
    x1, x2 = xf[..., :half], xf[..., half:]
    return jnp.concatenate([x1 * cos - x2 * sin, x2 * cos + x1 * sin], -1).astype(x.dtype)


def causal_dwconv(x, buf, w):
    n_taps, c = w.shape
    xp = jnp.concatenate([buf.astype(x.dtype), x], axis=1)
    y = lax.conv_general_dilated(xp, w.astype(x.dtype)[:, None, :], window_strides=(1,), padding='VALID',
                                 dimension_numbers=('NWC', 'WIO', 'NWC'), feature_group_count=c)
    return y, xp[:, xp.shape[1] - (n_taps - 1):]


def to_blocks(k, v):
    b, l, h, d = k.shape
    nb = -(-l // MOBA_BLOCK)
    pad = ((0, 0), (0, nb * MOBA_BLOCK - l), (0, 0), (0, 0))
    kb = jnp.pad(k, pad).reshape(b, nb, MOBA_BLOCK, h, d)
    vb = jnp.pad(v, pad).reshape(b, nb, MOBA_BLOCK, h, d)
    kmean = jnp.mean(kb.astype(jnp.float32), axis=2)
    return kb, vb, kmean


def moba_chunk(q, pos0, kb, vb, kmean):
    b, c, h, d = q.shape
    nb = kb.shape[1]
    n_sel = min(MOBA_TOPK, nb)
    f32 = jnp.float32
    qf = q.astype(f32) * (d ** -0.5)
    ob = pos0 // MOBA_BLOCK
    gate = jnp.einsum('bchd,bnhd->bchn', qf, kmean)
    gate = jnp.where(jnp.arange(nb) < ob, gate, NEG_INF)
    _, idx = lax.top_k(gate, n_sel)
    valid = idx < ob
    b_ix = jnp.arange(b)[:, None, None, None]
    h_ix = jnp.arange(h)[None, None, :, None]
    k_sel = kb[b_ix, idx, :, h_ix].astype(f32)
    v_sel = vb[b_ix, idx, :, h_ix].astype(f32)
    s_sel = jnp.einsum('bchd,bchjpd->bchjp', qf, k_sel)
    s_sel = jnp.where(valid[..., None], s_sel, NEG_INF).reshape(b, c, h, n_sel * MOBA_BLOCK)
    k_own = lax.dynamic_index_in_dim(kb, ob, axis=1, keepdims=False).astype(f32)
    v_own = lax.dynamic_index_in_dim(vb, ob, axis=1, keepdims=False).astype(f32)
    s_own = jnp.einsum('bchd,bphd->bchp', qf, k_own)
    q_pos = pos0 + jnp.arange(c)
    k_pos = ob * MOBA_BLOCK + jnp.arange(MOBA_BLOCK)
    causal = k_pos[None, :] <= q_pos[:, None]
    s_own = jnp.where(causal[None, :, None, :], s_own, NEG_INF)
    prob = jax.nn.softmax(jnp.concatenate([s_sel, s_own], -1), axis=-1)
    p_sel = prob[..., :n_sel * MOBA_BLOCK].reshape(b, c, h, n_sel, MOBA_BLOCK)
    p_own = prob[..., n_sel * MOBA_BLOCK:]
    out = jnp.einsum('bchjp,bchjpd->bchd', p_sel, v_sel) + jnp.einsum('bchp,bphd->bchd', p_own, v_own)
    return out.astype(q.dtype)


def moba_prompt(q, k, v):
    b, s, h, d = q.shape
    kb, vb, kmean = to_blocks(k, v)
    n_q = s // MOBA_QBLOCK
    qc = jnp.moveaxis(q.reshape(b, n_q, MOBA_QBLOCK, h, d), 1, 0)
    pos0 = jnp.arange(n_q, dtype=jnp.int32) * MOBA_QBLOCK
    out = lax.map(lambda a: moba_chunk(a[0], a[1], kb, vb, kmean), (qc, pos0))
    return jnp.moveaxis(out, 0, 1).reshape(b, s, h, d)


def rwkv7_time_mix(z, prev, s0, p):
    b, t, _ = z.shape
    g_w, h, n = GROUP_WIDTH, RWKV_HEADS, HEAD_DIM
    f32 = jnp.float32
    z_prev = jnp.concatenate([prev[:, None].astype(z.dtype), z[:, :-1]], axis=1)
    zs = (z + (z_prev - z) * p['rwkv_mu']).astype(f32)
    r, k, v, wd, ad, gd = jnp.split(zs, [g_w, 2 * g_w, 3 * g_w, 3 * g_w + RWKV_DECAY_RANK,
                                         3 * g_w + RWKV_DECAY_RANK + RWKV_AAA_RANK], axis=-1)
    w_log = -jax.nn.softplus(-(p['rwkv_w0'].astype(f32) + jnp.tanh(wd) @ p['rwkv_w2'].astype(f32))) - 0.5
    decay = jnp.exp(-jnp.exp(w_log))
    a = jax.nn.sigmoid(p['rwkv_a0'].astype(f32) + ad @ p['rwkv_a2'].astype(f32))
    g = jax.nn.sigmoid(gd) @ p['rwkv_g2'].astype(f32)
    heads = lambda u: u.reshape(b, t, h, n)
    kk = heads(k * p['rwkv_kk'].astype(f32))
    kk = kk / jnp.maximum(jnp.linalg.norm(kk, axis=-1, keepdims=True), 1e-12)
    k = k * (1.0 + (a - 1.0) * p['rwkv_ka'].astype(f32))
    r, k, v, a, decay = heads(r), heads(k), heads(v), heads(a), heads(decay)

    def step(s, inp):
        r_t, w_t, k_t, v_t, kk_t, a_t = inp
        sa = jnp.einsum('bhij,bhj->bhi', s, -kk_t)
        s = s * w_t[:, :, None, :] + sa[..., None] * (kk_t * a_t)[:, :, None, :] + v_t[..., None] * k_t[:, :, None, :]
        return s, jnp.einsum('bhij,bhj->bhi', s, r_t)

    xs = tuple(jnp.moveaxis(u, 1, 0) for u in (r, decay, k, v, kk, a))
    s_last, y = lax.scan(step, s0.astype(f32), xs)
    y = jnp.moveaxis(y, 0, 1)
    mu = jnp.mean(y, -1, keepdims=True)
    var = jnp.mean(jnp.square(y - mu), -1, keepdims=True)
    y = ((y - mu) * lax.rsqrt(var + RWKV_GN_EPS)).reshape(b, t, g_w) * p['rwkv_ln_g'].astype(f32) + p['rwkv_ln_b'].astype(f32)
    bonus = jnp.sum(r * k * p['rwkv_rk'].astype(f32), -1, keepdims=True) * v
    y = (y + bonus.reshape(b, t, g_w)) * g
    return y.astype(z.dtype), z[:, -1], s_last.astype(s0.dtype)


def peer_ffn(x, wq, k1, k2, u, v):
    b, t, dm = x.shape
    n = b * t
    blk = min(PEER_TOKBLOCK, n)
    n_blk = -(-n // blk)
    xt = jnp.pad(x.reshape(n, dm), ((0, n_blk * blk - n), (0, 0)))
    half = PEER_QDIM // 2

    def block(xc):
        q = (xc @ wq).reshape(blk, PEER_HEADS, PEER_QDIM)
        s1 = jnp.einsum('nhd,kd->nhk', q[..., :half], k1).astype(jnp.float32)
        s2 = jnp.einsum('nhd,kd->nhk', q[..., half:], k2).astype(jnp.float32)
        v1, i1 = lax.top_k(s1, PEER_TOPK)
        v2, i2 = lax.top_k(s2, PEER_TOPK)
        cand = (v1[..., :, None] + v2[..., None, :]).reshape(blk, PEER_HEADS, PEER_TOPK * PEER_TOPK)
        sc, ci = lax.top_k(cand, PEER_TOPK)
        e1 = jnp.take_along_axis(i1, ci // PEER_TOPK, axis=-1)
        e2 = jnp.take_along_axis(i2, ci % PEER_TOPK, axis=-1)
        eidx = e1 * PEER_NKEYS + e2
        gate = jax.nn.softmax(sc, axis=-1)
        act = jax.nn.gelu(jnp.einsum('nd,nhed->nhe', xc, u[eidx]).astype(jnp.float32), approximate=False)
        return jnp.einsum('nhe,nhed->nd', (gate * act).astype(xc.dtype), v[eidx])

    y = lax.map(block, xt.reshape(n_blk, blk, dm))
    return y.reshape(n_blk * blk, dm)[:n].reshape(b, t, dm)


def trunk_layer(x, pos0, k_past, v_past, conf_buf, shift_prev, wkv_prev, sc_buf, p):
    b, t, _ = x.shape
    pos = pos0 + jnp.arange(t)
    h = rms_norm(x, p['norm_mix'])
    proj = h @ p['w_in']
    att, conf, rw, sc = jnp.split(proj, [ATT_COLS, ATT_COLS + CONF_COLS, ATT_COLS + CONF_COLS + RWKV_PROJ], axis=-1)
    q, k, v = [u.reshape(b, t, ATT_HEADS, HEAD_DIM) for u in jnp.split(att, 3, axis=-1)]
    q = rope(rms_norm(q, p['q_norm']), pos)
    k = rope(rms_norm(k, p['k_norm']), pos)
    if k_past is None:
        o_a = moba_prompt(q, k, v)
    else:
        kb, vb, kmean = to_blocks(jnp.concatenate([k_past.astype(k.dtype), k], 1),
                                  jnp.concatenate([v_past.astype(v.dtype), v], 1))
        o_a = moba_chunk(q, pos0, kb, vb, kmean)
    o_a = o_a.reshape(b, t, GROUP_WIDTH)
    val, gt = jnp.split(conf, 2, axis=-1)
    c, conf_new = causal_dwconv(val * jax.nn.sigmoid(gt), conf_buf, p['conf_dw'])
    o_b = jax.nn.silu(layer_norm(c + p['conf_dw_b'], p['conf_ln_g'], p['conf_ln_b'], LN_EPS))
    o_c, shift_new, wkv_new = rwkv7_time_mix(rw, shift_prev, wkv_prev, p)
    b_gate, c_gate, hs = jnp.split(sc, 3, axis=-1)
    d, sc_new = causal_dwconv(c_gate * hs, sc_buf, p['sc_w'])
    o_d = b_gate * d
    x = x + jnp.concatenate([o_a, o_b.astype(x.dtype), o_c, o_d], -1) @ p['w_out']
    x = x + peer_ffn(rms_norm(x, p['norm_ffn']), p['peer_wq'], p['peer_k1'], p['peer_k2'], p['peer_u'], p['peer_v'])
    return x, k, v, conf_new, shift_new, wkv_new, sc_new


def setup_inputs(seed: int = 0) -> dict:
    key = jax.random.key(seed)
    keys = iter(jax.random.split(key, 48))
    f32 = jnp.float32
    nrm = lambda shape, scale: scale * jax.random.normal(next(keys), shape, f32)
    unif = lambda shape, lo, hi: jax.random.uniform(next(keys), shape, f32, lo, hi)
    n_pages = PAST_LEN // PAGE_SIZE
    n_used = DEC_BATCH * n_pages
    n_pool = n_used + n_used // 4
    g_w = GROUP_WIDTH
    x_prompt = nrm((BATCH, SEQ, D_MODEL), 1.0)
    x_sample = nrm((DEC_BATCH, DEC_SEQ, D_MODEL), 1.0)
    cache_k = nrm((DEPTH, n_pool, PAGE_SIZE, ATT_HEADS, HEAD_DIM), 1.0)
    cache_v = nrm((DEPTH, n_pool, PAGE_SIZE, ATT_HEADS, HEAD_DIM), 1.0)
    perm = jax.random.permutation(next(keys), n_pool)
    page_table = perm[:n_used].reshape(DEC_BATCH, n_pages).astype(jnp.int32)
    return {
        'x_prompt': x_prompt,
        'x_sample': x_sample,
        'cache_k': cache_k,
        'cache_v': cache_v,
        'page_table': page_table,
        'state_conf_conv': nrm((DEPTH, DEC_BATCH, CONF_TAPS - 1, g_w), 0.5),
        'state_rwkv_shift': nrm((DEPTH, DEC_BATCH, RWKV_PROJ), 1.0),
        'state_rwkv_wkv': nrm((DEPTH, DEC_BATCH, RWKV_HEADS, HEAD_DIM, HEAD_DIM), 0.5),
        'state_short_conv': nrm((DEPTH, DEC_BATCH, SC_TAPS - 1, g_w), 0.5),
        'norm_mix': 1.0 + nrm((DEPTH, D_MODEL), 0.02),
        'w_in': nrm((DEPTH, D_MODEL, IN_COLS), D_MODEL ** -0.5),
        'q_norm': 1.0 + nrm((DEPTH, HEAD_DIM), 0.02),
        'k_norm': 1.0 + nrm((DEPTH, HEAD_DIM), 0.02),
        'conf_dw': nrm((DEPTH, CONF_TAPS, g_w), CONF_TAPS ** -0.5),
        'conf_dw_b': nrm((DEPTH, g_w), 0.02),
        'conf_ln_g': 1.0 + nrm((DEPTH, g_w), 0.02),
        'conf_ln_b': nrm((DEPTH, g_w), 0.02),
        'rwkv_mu': unif((DEPTH, RWKV_PROJ), 0.0, 1.0),
        'rwkv_w0': unif((DEPTH, g_w), -4.0, -1.0),
        'rwkv_w2': nrm((DEPTH, RWKV_DECAY_RANK, g_w), 0.1),
        'rwkv_a0': nrm((DEPTH, g_w), 0.1),
        'rwkv_a2': nrm((DEPTH, RWKV_AAA_RANK, g_w), 0.3 * RWKV_AAA_RANK ** -0.5),
        'rwkv_g2': nrm((DEPTH, RWKV_GATE_RANK, g_w), RWKV_GATE_RANK ** -0.5),
        'rwkv_kk': 0.85 + nrm((DEPTH, g_w), 0.02),
        'rwkv_ka': 1.0 + nrm((DEPTH, g_w), 0.02),
        'rwkv_rk': nrm((DEPTH, RWKV_HEADS, HEAD_DIM), 0.1),
        'rwkv_ln_g': 1.0 + nrm((DEPTH, g_w), 0.02),
        'rwkv_ln_b': nrm((DEPTH, g_w), 0.02),
        'sc_w': nrm((DEPTH, SC_TAPS, g_w), SC_TAPS ** -0.5),
        'w_out': nrm((DEPTH, MIX_WIDTH, D_MODEL), MIX_WIDTH ** -0.5),
        'norm_ffn': 1.0 + nrm((DEPTH, D_MODEL), 0.02),
        'peer_wq': nrm((DEPTH, D_MODEL, PEER_HEADS * PEER_QDIM), D_MODEL ** -0.5),
        'peer_k1': nrm((DEPTH, PEER_NKEYS, PEER_QDIM // 2), (PEER_QDIM // 2) ** -0.5),
        'peer_k2': nrm((DEPTH, PEER_NKEYS, PEER_QDIM // 2), (PEER_QDIM // 2) ** -0.5),
        'peer_u': nrm((DEPTH, PEER_EXPERTS, D_MODEL), D_MODEL ** -0.5),
        'peer_v': nrm((DEPTH, PEER_EXPERTS, D_MODEL), PEER_HEADS ** -0.5),
    }


def reference(x_prompt, x_sample, cache_k, cache_v, page_table, state_conf_conv, state_rwkv_shift,
              state_rwkv_wkv, state_short_conv, norm_mix, w_in, q_norm, k_norm, conf_dw, conf_dw_b,
              conf_ln_g, conf_ln_b, rwkv_mu, rwkv_w0, rwkv_w2, rwkv_a0, rwkv_a2, rwkv_g2, rwkv_kk,
              rwkv_ka, rwkv_rk, rwkv_ln_g, rwkv_ln_b, sc_w, w_out, norm_ffn, peer_wq, peer_k1, peer_k2,
              peer_u, peer_v):
    bp = x_prompt.shape[0]
    db = x_sample.shape[0]
    dt = x_prompt.dtype
    yp, ys = x_prompt, x_sample
    outs_p = [[] for _ in range(6)]
    outs_s = [[] for _ in range(6)]
    for l in range(DEPTH):
        p = dict(norm_mix=norm_mix[l], w_in=w_in[l], q_norm=q_norm[l], k_norm=k_norm[l],
                 conf_dw=conf_dw[l], conf_dw_b=conf_dw_b[l], conf_ln_g=conf_ln_g[l], conf_ln_b=conf_ln_b[l],
                 rwkv_mu=rwkv_mu[l], rwkv_w0=rwkv_w0[l], rwkv_w2=rwkv_w2[l], rwkv_a0=rwkv_a0[l],
                 rwkv_a2=rwkv_a2[l], rwkv_g2=rwkv_g2[l], rwkv_kk=rwkv_kk[l], rwkv_ka=rwkv_ka[l],
                 rwkv_rk=rwkv_rk[l], rwkv_ln_g=rwkv_ln_g[l], rwkv_ln_b=rwkv_ln_b[l], sc_w=sc_w[l],
                 w_out=w_out[l], norm_ffn=norm_ffn[l], peer_wq=peer_wq[l], peer_k1=peer_k1[l],
                 peer_k2=peer_k2[l], peer_u=peer_u[l], peer_v=peer_v[l])
        yp, *new_p = trunk_layer(
            yp, 0, None, None,
            jnp.zeros((bp, CONF_TAPS - 1, GROUP_WIDTH), dt),
            jnp.zeros((bp, RWKV_PROJ), dt),
            jnp.zeros((bp, RWKV_HEADS, HEAD_DIM, HEAD_DIM), dt),
            jnp.zeros((bp, SC_TAPS - 1, GROUP_WIDTH), dt), p)
        k_past = cache_k[l][page_table].reshape(db, PAST_LEN, ATT_HEADS, HEAD_DIM)
        v_past = cache_v[l][page_table].reshape(db, PAST_LEN, ATT_HEADS, HEAD_DIM)
        ys, *new_s = trunk_layer(ys, PAST_LEN, k_past, v_past, state_conf_conv[l], state_rwkv_shift[l],
                                 state_rwkv_wkv[l], state_short_conv[l], p)
        for lst, a in zip(outs_p, new_p):
            lst.append(a)
        for lst, a in zip(outs_s, new_s):
            lst.append(a)
    pk, pv, pconf, pshift, pwkv, psc = [jnp.stack(a) for a in outs_p]
    sk, sv, sconf, sshift, swkv, ssc = [jnp.stack(a) for a in outs_s]
    return (yp, ys, pk, pv, sk, sv, pconf, sconf, pshift, sshift, pwkv, swkv, psc, ssc)
```

```python
import functools
import math

import jax
import jax.numpy as jnp
from jax import lax
from jax.experimental import pallas as pl
from jax.experimental.pallas import tpu as pltpu

F32 = jnp.float32
BF16 = jnp.bfloat16

HEAD_DIM = 64
GROUP = 256
N_HEADS = GROUP // HEAD_DIM
MOBA_BLOCK = 256
MOBA_TOPK = 3
ROPE_THETA = 10000.0
PAGE = 128
CONF_TAPS = 31
SC_TAPS = 3
RWKV_LOW = 128
PEER_HEADS = 8
PEER_NKEYS = 128
PEER_TOPK = 16
RMS_EPS = 1e-6
LN_EPS = 1e-5
RWKV_GN_EPS = 64e-5
NEG_INF = -1e30
LANES = 128
VMEM_LIMIT = 56 * 1024 * 1024

_ARB = "arbitrary"


def _cparams(n_axes, vmem=VMEM_LIMIT):
    return pltpu.CompilerParams(dimension_semantics=(_ARB,) * n_axes, vmem_limit_bytes=vmem)


def _mm(a, b):
    return jnp.dot(a.astype(BF16), b.astype(BF16), preferred_element_type=F32)


def _mm_nt(a, b):
    return lax.dot_general(a.astype(BF16), b.astype(BF16), (((1,), (1,)), ((), ())),
                           preferred_element_type=F32)


def _split(a):
    hi = a.astype(BF16)
    lo = (a - hi.astype(F32)).astype(BF16)
    return hi, lo


def _mm3(a, b):
    ah, al = _split(a)
    bh, bl = _split(b)
    d = functools.partial(jnp.dot, preferred_element_type=F32)
    return d(ah, bh) + d(ah, bl) + d(al, bh)


def _mm3_nt(a, b):
    ah, al = _split(a)
    bh, bl = _split(b)
    d = functools.partial(lax.dot_general, dimension_numbers=(((1,), (1,)), ((), ())),
                          preferred_element_type=F32)
    return d(ah, bh) + d(ah, bl) + d(al, bh)


def _mm2_exact_rhs(a, b_bf16):
    ah, al = _split(a)
    d = functools.partial(jnp.dot, preferred_element_type=F32)
    return d(ah, b_bf16) + d(al, b_bf16)


def _sigmoid(x):
    return 1.0 / (1.0 + jnp.exp(-x))


def _head_of_lane(shape, axis):
    return lax.broadcasted_iota(jnp.int32, shape, axis) // HEAD_DIM


def _head_sum_matrix():
    r = _head_of_lane((GROUP, GROUP), 0)
    c = _head_of_lane((GROUP, GROUP), 1)
    return jnp.where(r == c, 1.0, 0.0).astype(BF16)


def _proj_kernel(x_ref, g_ref, wa_ref, wc_ref, wr_ref, ws_ref, oa_ref, oc_ref, or_ref, os_ref):
    x = x_ref[...]
    ms = jnp.mean(x * x, axis=-1, keepdims=True)
    h = (x * lax.rsqrt(ms + RMS_EPS) * g_ref[...]).astype(BF16)
    for w_ref, o_ref in ((wa_ref, oa_ref), (wc_ref, oc_ref), (wr_ref, or_ref), (ws_ref, os_ref)):
        o_ref[...] = jnp.dot(h, w_ref[...], preferred_element_type=F32)


def _proj(x, g, ws, tm):
    n, d = x.shape
    widths = [w.shape[1] for w in ws]
    return pl.pallas_call(
        _proj_kernel,
        grid=(n // tm,),
        in_specs=[pl.BlockSpec((tm, d), lambda i: (i, 0)),
                  pl.BlockSpec((1, d), lambda i: (0, 0))]
                 + [pl.BlockSpec((d, wd), lambda i: (0, 0)) for wd in widths],
        out_specs=[pl.BlockSpec((tm, wd), lambda i: (i, 0)) for wd in widths],
        out_shape=[jax.ShapeDtypeStruct((n, wd), F32) for wd in widths],
        compiler_params=_cparams(1),
        name="proj",
    )(x, g, *ws)


def _head_rms(x, gain, bd):
    ms = _mm2_exact_rhs(x * x, bd) * (1.0 / HEAD_DIM)
    return x * lax.rsqrt(ms + RMS_EPS) * gain


def _rope(x, cos, sin_signed):
    lane = lax.broadcasted_iota(jnp.int32, x.shape, 1) % HEAD_DIM
    lo = pltpu.roll(x, GROUP - HEAD_DIM // 2, axis=1)
    hi = pltpu.roll(x, HEAD_DIM // 2, axis=1)
    rot = jnp.where(lane < HEAD_DIM // 2, lo, hi)
    return x * cos + rot * sin_signed


def _top_rows(g, n_rows, k):
    ridx = lax.broadcasted_iota(jnp.int32, g.shape, 0).astype(F32)
    sel = jnp.zeros(g.shape, F32)
    cur = g
    for _ in range(k):
        mx = jnp.max(cur, axis=0, keepdims=True)
        first = jnp.min(jnp.where(cur == mx, ridx, float(n_rows)), axis=0, keepdims=True)
        hit = ridx == first
        sel = jnp.where(hit, 1.0, sel)
        cur = jnp.where(hit, -jnp.inf, cur)
    return sel


def _moba_prompt_kernel(att_ref, cos_ref, sin_ref, qg_ref, kg_ref, oa_ref, kr_ref,
                        kb_ref, vt_ref, km_ref, sel_ref, acc_ref, m_ref, l_ref, qm_ref):
    qi = pl.program_id(1)
    nblk = kb_ref.shape[0]
    blk = MOBA_BLOCK
    bd = _head_sum_matrix()

    @pl.when(qi == 0)
    def _():
        km_ref[...] = jnp.zeros(km_ref.shape, F32)

    a = att_ref[0]
    cos = cos_ref[...]
    sin = sin_ref[...]
    qr = _rope(_head_rms(a[:, 0:GROUP], qg_ref[...], bd), cos, sin)
    kr = _rope(_head_rms(a[:, GROUP:2 * GROUP], kg_ref[...], bd), cos, sin)
    v = a[:, 2 * GROUP:3 * GROUP]
    kr_ref[0] = kr
    kb_ref[qi] = kr.astype(BF16)
    vt_ref[qi] = v.T.astype(BF16)
    km_ref[pl.ds(qi, 1), :] = jnp.mean(kr, axis=0, keepdims=True)

    qs = qr * (HEAD_DIM ** -0.5)
    lane_head = _head_of_lane((blk, GROUP), 1)
    km = km_ref[...]
    km_head = _head_of_lane(km.shape, 1)
    blk_row = lax.broadcasted_iota(jnp.int32, (nblk, blk), 0)
    krow = lax.broadcasted_iota(jnp.int32, (blk, blk), 0)
    qcol = lax.broadcasted_iota(jnp.int32, (blk, blk), 1)
    kcur = kb_ref[qi]
    for h in range(N_HEADS):
        qm = jnp.where(lane_head == h, qs, 0.0)
        qm_ref[h] = qm.astype(BF16)
        gate = _mm3_nt(jnp.where(km_head == h, km, 0.0), qm)
        past = blk_row < qi
        sel = _top_rows(jnp.where(past, gate, NEG_INF), nblk, MOBA_TOPK)
        sel_ref[h] = jnp.where(past, sel, 0.0)
        st = lax.dot_general(kcur, qm_ref[h], (((1,), (1,)), ((), ())), preferred_element_type=F32)
        st = jnp.where(krow <= qcol, st, NEG_INF)
        m = jnp.max(st, axis=0, keepdims=True)
        p = jnp.exp(st - m)
        m_ref[pl.ds(h, 1), :] = m
        l_ref[pl.ds(h, 1), :] = jnp.sum(p, axis=0, keepdims=True)
        acc_ref[h * HEAD_DIM:(h + 1) * HEAD_DIM, :] = jnp.dot(
            vt_ref[qi, h * HEAD_DIM:(h + 1) * HEAD_DIM, :], p.astype(BF16), preferred_element_type=F32)

    def past_block(j, carry):
        kj = kb_ref[j]
        for h in range(N_HEADS):
            st = lax.dot_general(kj, qm_ref[h], (((1,), (1,)), ((), ())), preferred_element_type=F32)
            chosen = sel_ref[h, pl.ds(j, 1), :] > 0.0
            st = jnp.where(chosen, st, NEG_INF)
            m_old = m_ref[pl.ds(h, 1), :]
            m_new = jnp.maximum(m_old, jnp.max(st, axis=0, keepdims=True))
            alpha = jnp.exp(m_old - m_new)
            p = jnp.exp(st - m_new)
            m_ref[pl.ds(h, 1), :] = m_new
            l_ref[pl.ds(h, 1), :] = alpha * l_ref[pl.ds(h, 1), :] + jnp.sum(p, axis=0, keepdims=True)
            rows = slice(h * HEAD_DIM, (h + 1) * HEAD_DIM)
            acc_ref[rows, :] = alpha * acc_ref[rows, :] + jnp.dot(
                vt_ref[j, rows, :], p.astype(BF16), preferred_element_type=F32)
        return carry

    lax.fori_loop(0, qi, past_block, 0)

    parts = []
    for h in range(N_HEADS):
        rows = slice(h * HEAD_DIM, (h + 1) * HEAD_DIM)
        parts.append(acc_ref[rows, :] / l_ref[pl.ds(h, 1), :])
    oa_ref[0] = jnp.concatenate(parts, axis=0).T


def _moba_prompt(att, cos, sin, qg, kg):
    b, t, _ = att.shape
    nblk = t // MOBA_BLOCK
    blk = MOBA_BLOCK
    return pl.pallas_call(
        _moba_prompt_kernel,
        grid=(b, nblk),
        in_specs=[pl.BlockSpec((1, blk, 3 * GROUP), lambda i, j: (i, j, 0)),
                  pl.BlockSpec((blk, GROUP), lambda i, j: (j, 0)),
                  pl.BlockSpec((blk, GROUP), lambda i, j: (j, 0)),
                  pl.BlockSpec((1, GROUP), lambda i, j: (0, 0)),
                  pl.BlockSpec((1, GROUP), lambda i, j: (0, 0))],
        out_specs=[pl.BlockSpec((1, blk, GROUP), lambda i, j: (i, j, 0)),
                   pl.BlockSpec((1, blk, GROUP), lambda i, j: (i, j, 0))],
        out_shape=[jax.ShapeDtypeStruct((b, t, GROUP), F32),
                   jax.ShapeDtypeStruct((b, t, GROUP), F32)],
        scratch_shapes=[pltpu.VMEM((nblk, blk, GROUP), BF16),
                        pltpu.VMEM((nblk, GROUP, blk), BF16),
                        pltpu.VMEM((nblk, GROUP), F32),
                        pltpu.VMEM((N_HEADS, nblk, blk), F32),
                        pltpu.VMEM((GROUP, blk), F32),
                        pltpu.VMEM((8, blk), F32),
                        pltpu.VMEM((8, blk), F32),
                        pltpu.VMEM((N_HEADS, blk, GROUP), BF16)],
        compiler_params=_cparams(2),
        name="moba_prompt",
    )(att, cos, sin, qg, kg)


def _decode_sel_kernel(pt_ref, att_ref, cos_ref, sin_ref, qg_ref, kg_ref, *rest, pages_per_step):
    page_refs = rest[:pages_per_step]
    sel_ref, qr_ref, kr_ref, km_ref = rest[pages_per_step:]
    s = pl.program_id(1)
    n_steps = pl.num_programs(1)
    blocks_per_step = pages_per_step * PAGE // MOBA_BLOCK
    n_past = n_steps * blocks_per_step
    pages_per_block = MOBA_BLOCK // PAGE
    bd = _head_sum_matrix()

    @pl.when(s == 0)
    def _():
        km_ref[...] = jnp.zeros(km_ref.shape, F32)
        a = att_ref[0]
        qr_ref[0] = _rope(_head_rms(a[:, 0:GROUP], qg_ref[...], bd), cos_ref[...], sin_ref[...])
        kr_ref[0] = _rope(_head_rms(a[:, GROUP:2 * GROUP], kg_ref[...], bd), cos_ref[...], sin_ref[...])

    for bi in range(blocks_per_step):
        tot = jnp.zeros((1, GROUP), F32)
        for pi in range(pages_per_block):
            tot = tot + jnp.sum(page_refs[bi * pages_per_block + pi][0], axis=0, keepdims=True)
        km_ref[pl.ds(s * blocks_per_step + bi, 1), :] = tot * (1.0 / MOBA_BLOCK)

    @pl.when(s == n_steps - 1)
    def _():
        qs = qr_ref[0] * (HEAD_DIM ** -0.5)
        km = km_ref[...]
        lane_head = _head_of_lane(qs.shape, 1)
        gates = []
        for h in range(N_HEADS):
            gates.append(_mm3_nt(jnp.where(lane_head == h, qs, 0.0), km))
        gate = jnp.concatenate(gates, axis=0)
        lane = lax.broadcasted_iota(jnp.int32, gate.shape, 1)
        past = lane < n_past
        cur = jnp.where(past, gate, NEG_INF)
        lidx = lane.astype(F32)
        sel = jnp.zeros(gate.shape, F32)
        for _ in range(MOBA_TOPK):
            mx = jnp.max(cur, axis=1, keepdims=True)
            first = jnp.min(jnp.where(cur == mx, lidx, float(LANES)), axis=1, keepdims=True)
            hit = lidx == first
            sel = jnp.where(hit, 1.0, sel)
            cur = jnp.where(hit, -jnp.inf, cur)
        sel_ref[0] = jnp.where(past, sel, 0.0)


def _decode_sel(att, cos, sin, qg, kg, cache_k, page_table, layer_off, pages_per_step=4):
    db, t, _ = att.shape
    n_pages = page_table.shape[1]
    n_steps = n_pages // pages_per_step
    assert n_pages * PAGE // MOBA_BLOCK <= LANES

    def page_map(i):
        return lambda b, s, pt: (pt[b, s * pages_per_step + i] + layer_off, 0, 0)

    grid_spec = pltpu.PrefetchScalarGridSpec(
        num_scalar_prefetch=1,
        grid=(db, n_steps),
        in_specs=[pl.BlockSpec((1, t, 3 * GROUP), lambda b, s, pt: (b, 0, 0)),
                  pl.BlockSpec((t, GROUP), lambda b, s, pt: (0, 0)),
                  pl.BlockSpec((t, GROUP), lambda b, s, pt: (0, 0)),
                  pl.BlockSpec((1, GROUP), lambda b, s, pt: (0, 0)),
                  pl.BlockSpec((1, GROUP), lambda b, s, pt: (0, 0))]
                 + [pl.BlockSpec((1, PAGE, GROUP), page_map(i)) for i in range(pages_per_step)],
        out_specs=[pl.BlockSpec((1, N_HEADS * t, LANES), lambda b, s, pt: (b, 0, 0)),
                   pl.BlockSpec((1, t, GROUP), lambda b, s, pt: (b, 0, 0)),
                   pl.BlockSpec((1, t, GROUP), lambda b, s, pt: (b, 0, 0))],
        scratch_shapes=[pltpu.VMEM((LANES, GROUP), F32)],
    )
    return pl.pallas_call(
        functools.partial(_decode_sel_kernel, pages_per_step=pages_per_step),
        grid_spec=grid_spec,
        out_shape=[jax.ShapeDtypeStruct((db, N_HEADS * t, LANES), F32),
                   jax.ShapeDtypeStruct((db, t, GROUP), F32),
                   jax.ShapeDtypeStruct((db, t, GROUP), F32)],
        compiler_params=_cparams(2),
        name="decode_sel",
    )(page_table, att, cos, sin, qg, kg, *([cache_k] * pages_per_step))


def _decode_att_kernel(pt_ref, sel_ref, qr_ref, kn_ref, vn_ref, k0_ref, k1_ref, v0_ref, v1_ref,
                       oa_ref, qbd_ref, acc_ref, m_ref, l_ref):
    j = pl.program_id(1)
    n_blk = pl.num_programs(1)
    t = qr_ref.shape[1]
    rows = N_HEADS * t

    def online_update(st, vb):
        m_old = m_ref[...]
        m_new = jnp.maximum(m_old, jnp.max(st, axis=1, keepdims=True))
        alpha = jnp.exp(m_old - m_new)
        p = jnp.exp(st - m_new)
        m_ref[...] = m_new
        l_ref[...] = alpha * l_ref[...] + jnp.sum(p, axis=1, keepdims=True)
        acc_ref[...] = alpha * acc_ref[...] + jnp.dot(p.astype(BF16), vb, preferred_element_type=F32)

    @pl.when(j == 0)
    def _():
        qs = qr_ref[0] * (HEAD_DIM ** -0.5)
        lane_head = _head_of_lane(qs.shape, 1)
        qbd = jnp.concatenate([jnp.where(lane_head == h, qs, 0.0) for h in range(N_HEADS)], axis=0)
        qbd_ref[...] = qbd.astype(BF16)
        pad = jnp.zeros((MOBA_BLOCK - t, GROUP), F32)
        kn = jnp.concatenate([kn_ref[0], pad], axis=0).astype(BF16)
        vn = jnp.concatenate([vn_ref[0], pad], axis=0).astype(BF16)
        st = lax.dot_general(qbd_ref[...], kn, (((1,), (1,)), ((), ())), preferred_element_type=F32)
        qpos = lax.broadcasted_iota(jnp.int32, st.shape, 0) % t
        kpos = lax.broadcasted_iota(jnp.int32, st.shape, 1)
        st = jnp.where(kpos <= qpos, st, NEG_INF)
        m = jnp.max(st, axis=1, keepdims=True)
        p = jnp.exp(st - m)
        m_ref[...] = m
        l_ref[...] = jnp.sum(p, axis=1, keepdims=True)
        acc_ref[...] = jnp.dot(p.astype(BF16), vn, preferred_element_type=F32)

    kb = jnp.concatenate([k0_ref[0], k1_ref[0]], axis=0).astype(BF16)
    vb = jnp.concatenate([v0_ref[0], v1_ref[0]], axis=0).astype(BF16)
    st = lax.dot_general(qbd_ref[...], kb, (((1,), (1,)), ((), ())), preferred_element_type=F32)
    sel = sel_ref[0]
    lane = lax.broadcasted_iota(jnp.int32, sel.shape, 1)
    chosen = jnp.sum(jnp.where(lane == j, sel, 0.0), axis=1, keepdims=True) > 0.0
    online_update(jnp.where(chosen, st, NEG_INF), vb)

    @pl.when(j == n_blk - 1)
    def _():
        o = acc_ref[...] / l_ref[...]
        lane_head = _head_of_lane((t, GROUP), 1)
        out = jnp.zeros((t, GROUP), F32)
        for h in range(N_HEADS):
            out = out + jnp.where(lane_head == h, o[h * t:(h + 1) * t, :], 0.0)
        oa_ref[0] = out


def _decode_att(sel, qr, kn, vn, cache_k, cache_v, page_table, layer_off):
    db, t, _ = qr.shape
    n_blk = page_table.shape[1] * PAGE // MOBA_BLOCK
    rows = N_HEADS * t

    def page_map(i):
        return lambda b, j, pt: (pt[b, 2 * j + i] + layer_off, 0, 0)

    grid_spec = pltpu.PrefetchScalarGridSpec(
        num_scalar_prefetch=1,
        grid=(db, n_blk),
        in_specs=[pl.BlockSpec((1, rows, LANES), lambda b, j, pt: (b, 0, 0)),
                  pl.BlockSpec((1, t, GROUP), lambda b, j, pt: (b, 0, 0)),
                  pl.BlockSpec((1, t, GROUP), lambda b, j, pt: (b, 0, 0)),
                  pl.BlockSpec((1, t, GROUP), lambda b, j, pt: (b, 0, 0)),
                  pl.BlockSpec((1, PAGE, GROUP), page_map(0)),
                  pl.BlockSpec((1, PAGE, GROUP), page_map(1)),
                  pl.BlockSpec((1, PAGE, GROUP), page_map(0)),
                  pl.BlockSpec((1, PAGE, GROUP), page_map(1))],
        out_specs=[pl.BlockSpec((1, t, GROUP), lambda b, j, pt: (b, 0, 0))],
        scratch_shapes=[pltpu.VMEM((rows, GROUP), BF16),
                        pltpu.VMEM((rows, GROUP), F32),
                        pltpu.VMEM((rows, 1), F32),
                        pltpu.VMEM((rows, 1), F32)],
    )
    return pl.pallas_call(
        _decode_att_kernel,
        grid_spec=grid_spec,
        out_shape=[jax.ShapeDtypeStruct((db, t, GROUP), F32)],
        compiler_params=_cparams(2),
        name="decode_att",
    )(page_table, sel, qr, kn, vn, cache_k, cache_k, cache_v, cache_v)[0]


def _conv_kernel(conf_ref, sc_ref, cbuf_ref, sbuf_ref, dw_ref, dwb_ref, lng_ref, lnb_ref, scw_ref,
                 ob_ref, od_ref, cnew_ref, snew_ref, extc_ref, exts_ref):
    ti = pl.program_id(1)
    nt = pl.num_programs(1)
    tt = conf_ref.shape[1]
    hc = 32
    hs = 8
    nbc = CONF_TAPS - 1
    nbs = SC_TAPS - 1

    @pl.when(ti == 0)
    def _():
        extc_ref[0:hc - nbc, :] = jnp.zeros((hc - nbc, GROUP), F32)
        extc_ref[hc - nbc:hc, :] = cbuf_ref[0]
        exts_ref[0:hs - nbs, :] = jnp.zeros((hs - nbs, GROUP), F32)
        exts_ref[hs - nbs:hs, :] = sbuf_ref[0]

    c = conf_ref[0]
    glu = c[:, 0:GROUP] * _sigmoid(c[:, GROUP:2 * GROUP])
    extc_ref[hc:hc + tt, :] = glu
    acc = jnp.zeros((tt, GROUP), F32) + dwb_ref[...]
    for j in range(CONF_TAPS):
        acc = acc + dw_ref[j:j + 1, :] * extc_ref[pl.ds(hc - nbc + j, tt), :]
    mu = jnp.mean(acc, axis=-1, keepdims=True)
    var = jnp.mean(jnp.square(acc - mu), axis=-1, keepdims=True)
    ln = (acc - mu) * lax.rsqrt(var + LN_EPS) * lng_ref[...] + lnb_ref[...]
    ob_ref[0] = ln * _sigmoid(ln)

    s = sc_ref[0]
    gated = s[:, GROUP:2 * GROUP] * s[:, 2 * GROUP:3 * GROUP]
    exts_ref[hs:hs + tt, :] = gated
    d = jnp.zeros((tt, GROUP), F32)
    for j in range(SC_TAPS):
        d = d + scw_ref[j:j + 1, :] * exts_ref[pl.ds(hs - nbs + j, tt), :]
    od_ref[0] = s[:, 0:GROUP] * d

    @pl.when(ti == nt - 1)
    def _():
        cnew_ref[0] = extc_ref[pl.ds(tt + hc - nbc, nbc), :]
        snew_ref[0] = exts_ref[pl.ds(tt + hs - nbs, nbs), :]

    tail_c = extc_ref[pl.ds(tt, hc), :]
    tail_s = exts_ref[pl.ds(tt, hs), :]
    extc_ref[0:hc, :] = tail_c
    exts_ref[0:hs, :] = tail_s


def _conv(conf, sc, cbuf, sbuf, dw, dwb, lng, lnb, scw, tt):
    b, t, _ = conf.shape
    nbc = CONF_TAPS - 1
    nbs = SC_TAPS - 1
    const = lambda i, j: (0, 0)
    return pl.pallas_call(
        _conv_kernel,
        grid=(b, t // tt),
        in_specs=[pl.BlockSpec((1, tt, 2 * GROUP), lambda i, j: (i, j, 0)),
                  pl.BlockSpec((1, tt, 3 * GROUP), lambda i, j: (i, j, 0)),
                  pl.BlockSpec((1, nbc, GROUP), lambda i, j: (i, 0, 0)),
                  pl.BlockSpec((1, nbs, GROUP), lambda i, j: (i, 0, 0)),
                  pl.BlockSpec((CONF_TAPS, GROUP), const),
                  pl.BlockSpec((1, GROUP), const),
                  pl.BlockSpec((1, GROUP), const),
                  pl.BlockSpec((1, GROUP), const),
                  pl.BlockSpec((SC_TAPS, GROUP), const)],
        out_specs=[pl.BlockSpec((1, tt, GROUP), lambda i, j: (i, j, 0)),
                   pl.BlockSpec((1, tt, GROUP), lambda i, j: (i, j, 0)),
                   pl.BlockSpec((1, nbc, GROUP), lambda i, j: (i, 0, 0)),
                   pl.BlockSpec((1, nbs, GROUP), lambda i, j: (i, 0, 0))],
        out_shape=[jax.ShapeDtypeStruct((b, t, GROUP), F32),
                   jax.ShapeDtypeStruct((b, t, GROUP), F32),
                   jax.ShapeDtypeStruct((b, nbc, GROUP), F32),
                   jax.ShapeDtypeStruct((b, nbs, GROUP), F32)],
        scratch_shapes=[pltpu.VMEM((32 + tt, GROUP), F32),
                        pltpu.VMEM((8 + tt, GROUP), F32)],
        compiler_params=_cparams(2),
        name="conv",
    )(conf, sc, cbuf, sbuf, dw, dwb, lng, lnb, scw)


def _rwkv_kernel(rw_ref, shift_ref, st0_ref, mu_ref, w0_ref, a0_ref, kkw_ref, ka_ref, rk_ref,
                 lng_ref, lnb_ref, w2_ref, a2_ref, g2_ref, oc_ref, sto_ref, st_ref, prev_ref):
    c = pl.program_id(1)
    nc = pl.num_programs(1)
    L = rw_ref.shape[1]
    SL = max(N_HEADS * L, LANES)
    bd = _head_sum_matrix()

    @pl.when(c == 0)
    def _():
        st_ref[...] = st0_ref[0]
        prev_ref[...] = shift_ref[0]

    z = rw_ref[0]
    row = lax.broadcasted_iota(jnp.int32, z.shape, 0)
    zprev = jnp.where(row == 0, prev_ref[...], pltpu.roll(z, 1, axis=0))
    prev_ref[...] = z[L - 1:L, :]
    zs = z + (zprev - z) * mu_ref[...]
    r = zs[:, 0:GROUP]
    k = zs[:, GROUP:2 * GROUP]
    v = zs[:, 2 * GROUP:3 * GROUP]
    low = zs[:, 3 * GROUP:3 * GROUP + RWKV_LOW]

    x = -(w0_ref[...] + _mm3(jnp.tanh(low), w2_ref[...]))
    softplus = jnp.maximum(x, 0.0) + jnp.log(1.0 + jnp.exp(-jnp.abs(x)))
    ld = -jnp.exp(-softplus - 0.5)
    a = _sigmoid(a0_ref[...] + _mm3(low, a2_ref[...]))
    g = _mm(_sigmoid(low), g2_ref[...])
    kk = k * kkw_ref[...]
    kk = kk / jnp.maximum(jnp.sqrt(_mm2_exact_rhs(kk * kk, bd)), 1e-12)
    k2 = k * (1.0 + (a - 1.0) * ka_ref[...])

    trow = lax.broadcasted_iota(jnp.int32, (L, L), 0)
    tcol = lax.broadcasted_iota(jnp.int32, (L, L), 1)
    tri = jnp.where(trow >= tcol, 1.0, 0.0).astype(BF16)
    cum = _mm2_exact_rhs_lhs(tri, ld)
    cend = cum[L - 1:L, :]
    g_prev = jnp.exp(cum - ld)
    g_inv = jnp.exp(-cum)
    g_rel = jnp.exp(cend - cum)
    kka = kk * a

    lane_head = _head_of_lane((L, GROUP), 1)

    def stack(u):
        parts = [jnp.where(lane_head == h, u, 0.0) for h in range(N_HEADS)]
        if SL > N_HEADS * L:
            parts.append(jnp.zeros((SL - N_HEADS * L, GROUP), F32))
        return jnp.concatenate(parts, axis=0)

    at_s = stack(-kk * g_prev)
    bt_s = stack(kka * g_inv)
    kt_s = stack(k2 * g_inv)
    rt_s = stack(r * jnp.exp(cum))
    bg_s = stack(kka * g_rel)
    kg_s = stack(k2 * g_rel)
    v_s = stack(v)

    srow = lax.broadcasted_iota(jnp.int32, (SL, SL), 0)
    scol = lax.broadcasted_iota(jnp.int32, (SL, SL), 1)
    same = (srow // L) == (scol // L)
    strict = same & ((srow % L) > (scol % L))
    incl = same & ((srow % L) >= (scol % L))
    n_ab = jnp.where(strict, _mm_nt(at_s, bt_s), 0.0)
    n_ak = jnp.where(strict, _mm_nt(at_s, kt_s), 0.0)
    q_rb = jnp.where(incl, _mm_nt(rt_s, bt_s), 0.0)
    q_rk = jnp.where(incl, _mm_nt(rt_s, kt_s), 0.0)

    tm = jnp.where(srow == scol, 1.0, 0.0) + n_ab
    pw = n_ab
    for _ in range(int(math.log2(L)) - 1):
        pw = _mm(pw, pw)
        tm = tm + _mm(tm, pw)

    p1 = _mm(tm, at_s)
    p2 = _mm(tm, _mm(n_ak, v_s))
    bg_t = bg_s.T
    kg_t = kg_s.T
    drow = lax.broadcasted_iota(jnp.int32, (GROUP, GROUP), 0)
    dcol = lax.broadcasted_iota(jnp.int32, (GROUP, GROUP), 1)
    m_mat = jnp.where(drow == dcol, jnp.exp(cend), 0.0) + _mm3(bg_t, p1)
    c_mat = _mm3(bg_t, p2) + _mm3(kg_t, v_s)
    yc = rt_s + _mm(q_rb, p1)
    y0 = _mm(q_rb, p2) + _mm(q_rk, v_s)

    st = st_ref[...]
    y_s = _mm3(yc, st) + y0
    st_new = _mm3(m_mat, st) + c_mat
    st_ref[...] = st_new

    y = y_s[0:L, :]
    for h in range(1, N_HEADS):
        y = y + y_s[h * L:(h + 1) * L, :]

    inv_n = 1.0 / HEAD_DIM
    mean = _mm2_exact_rhs(y, bd) * inv_n
    yc0 = y - mean
    var = _mm2_exact_rhs(yc0 * yc0, bd) * inv_n
    yn = yc0 * lax.rsqrt(var + RWKV_GN_EPS) * lng_ref[...] + lnb_ref[...]
    bonus = _mm2_exact_rhs(r * k2 * rk_ref[...], bd) * v
    oc_ref[0] = (yn + bonus) * g

    @pl.when(c == nc - 1)
    def _():
        sto_ref[0] = st_new


def _mm2_exact_rhs_lhs(a_bf16, b):
    bh, bl = _split(b)
    d = functools.partial(jnp.dot, preferred_element_type=F32)
    return d(a_bf16, bh) + d(a_bf16, bl)


def _rwkv(rw, shift, st0, p, L):
    b, t, w = rw.shape
    const = lambda i, j: (0, 0)
    vec = pl.BlockSpec((1, GROUP), const)
    low = pl.BlockSpec((RWKV_LOW, GROUP), const)
    return pl.pallas_call(
        _rwkv_kernel,
        grid=(b, t // L),
        in_specs=[pl.BlockSpec((1, L, w), lambda i, j: (i, j, 0)),
                  pl.BlockSpec((1, 1, w), lambda i, j: (i, 0, 0)),
                  pl.BlockSpec((1, GROUP, GROUP), lambda i, j: (i, 0, 0)),
                  pl.BlockSpec((1, w), const),
                  vec, vec, vec, vec, vec, vec, vec, low, low, low],
        out_specs=[pl.BlockSpec((1, L, GROUP), lambda i, j: (i, j, 0)),
                   pl.BlockSpec((1, GROUP, GROUP), lambda i, j: (i, 0, 0))],
        out_shape=[jax.ShapeDtypeStruct((b, t, GROUP), F32),
                   jax.ShapeDtypeStruct((b, GROUP, GROUP), F32)],
        scratch_shapes=[pltpu.VMEM((GROUP, GROUP), F32),
                        pltpu.VMEM((1, w), F32)],
        compiler_params=_cparams(2),
        name="rwkv",
    )(rw, shift, st0, p["mu"], p["w0"], p["a0"], p["kkw"], p["ka"], p["rk"], p["lng"], p["lnb"],
      p["w2"], p["a2"], p["g2"])


def _state_to_blockdiag(s):
    b = s.shape[0]
    st = jnp.swapaxes(s, -1, -2)
    eye = jnp.eye(N_HEADS, dtype=s.dtype)
    return jnp.einsum("bhkv,hg->bhkgv", st, eye).reshape(b, GROUP, GROUP)


def _blockdiag_to_state(st):
    b = st.shape[0]
    x = st.reshape(b, N_HEADS, HEAD_DIM, N_HEADS, HEAD_DIM)
    diag = jnp.stack([x[:, h, :, h, :] for h in range(N_HEADS)], axis=1)
    return jnp.swapaxes(diag, -1, -2)


def _outproj_kernel(oa_ref, ob_ref, oc_ref, od_ref, x_ref, wo_ref, g_ref, wq_ref, x1_ref, h_ref, q_ref):
    acc = x_ref[...]
    for i, o_ref in enumerate((oa_ref, ob_ref, oc_ref, od_ref)):
        acc = acc + jnp.dot(o_ref[...].astype(BF16), wo_ref[i], preferred_element_type=F32)
    x1_ref[...] = acc
    ms = jnp.mean(acc * acc, axis=-1, keepdims=True)
    h = (acc * lax.rsqrt(ms + RMS_EPS) * g_ref[...]).astype(BF16)
    h_ref[...] = h
    q_ref[...] = jnp.dot(h, wq_ref[...], preferred_element_type=F32)


def _outproj(oa, ob, oc, od, x, wo, g, wq, tm):
    n, d = x.shape
    dq = wq.shape[1]
    mix = pl.BlockSpec((tm, GROUP), lambda i: (i, 0))
    return pl.pallas_call(
        _outproj_kernel,
        grid=(n // tm,),
        in_specs=[mix, mix, mix, mix,
                  pl.BlockSpec((tm, d), lambda i: (i, 0)),
                  pl.BlockSpec((4, GROUP, d), lambda i: (0, 0, 0)),
                  pl.BlockSpec((1, d), lambda i: (0, 0)),
                  pl.BlockSpec((d, dq), lambda i: (0, 0))],
        out_specs=[pl.BlockSpec((tm, d), lambda i: (i, 0)),
                   pl.BlockSpec((tm, d), lambda i: (i, 0)),
                   pl.BlockSpec((tm, dq), lambda i: (i, 0))],
        out_shape=[jax.ShapeDtypeStruct((n, d), F32),
                   jax.ShapeDtypeStruct((n, d), BF16),
                   jax.ShapeDtypeStruct((n, dq), F32)],
        compiler_params=_cparams(1),
        name="outproj",
    )(oa, ob, oc, od, x, wo, g, wq)


_CAND_PIECES = ((0, 0, 8), (0, 8, 8), (1, 0, 8), (2, 0, 5), (3, 0, 4), (4, 0, 3), (5, 0, 2),
                (6, 0, 2), (7, 0, 2))
_CAND_ROWS = 8 * (len(_CAND_PIECES) + 1)


def _peer_select_kernel(q_ref, k1_ref, k2_ref, s1_ref, s2_ref, stat_ref, xs_ref, v1_ref, v2_ref, cv_ref):
    tn = q_ref.shape[0]
    q = q_ref[...].astype(BF16)
    nt = functools.partial(lax.dot_general, dimension_numbers=(((1,), (1,)), ((), ())),
                           preferred_element_type=F32)
    s1 = nt(k1_ref[...], q)
    s2 = nt(k2_ref[...], q)
    s1_ref[0] = s1
    s2_ref[0] = s2

    def extract(n_rows, vals_ref):
        ridx = lax.broadcasted_iota(jnp.int32, (n_rows, tn), 0).astype(F32)

        def body(r, carry):
            x = xs_ref[0:n_rows, :]
            mx = jnp.max(x, axis=0, keepdims=True)
            first = jnp.min(jnp.where(x == mx, ridx, float(n_rows)), axis=0, keepdims=True)
            xs_ref[0:n_rows, :] = jnp.where(ridx == first, -jnp.inf, x)
            vals_ref[pl.ds(r, 1), :] = mx
            return carry

        lax.fori_loop(0, PEER_TOPK, body, 0)

    xs_ref[...] = s1
    extract(PEER_NKEYS, v1_ref)
    xs_ref[...] = s2
    extract(PEER_NKEYS, v2_ref)

    v1 = v1_ref[...]
    v2 = v2_ref[...]
    sub = lax.broadcasted_iota(jnp.int32, (8, tn), 0)
    for i, (a, b0, nb) in enumerate(_CAND_PIECES):
        piece = v1[a:a + 1, :] + v2[b0:b0 + 8, :]
        xs_ref[8 * i:8 * i + 8, :] = jnp.where(sub < nb, piece, -jnp.inf)
    i = len(_CAND_PIECES)
    xs_ref[8 * i:8 * i + 8, :] = v1[8:16, :] + v2[0:1, :]
    extract(_CAND_ROWS, cv_ref)

    cv = cv_ref[...]
    top = cv[0:1, :]
    z = jnp.sum(jnp.exp(cv - top), axis=0, keepdims=True)
    stat_ref[0] = jnp.concatenate(
        [cv[PEER_TOPK - 1:PEER_TOPK, :], v1[0:1, :], v2[0:1, :], 1.0 / z, jnp.zeros((4, tn), F32)], axis=0)


def _peer_select(q, k1p, k2p, tn):
    n, dq = q.shape
    qd = dq // PEER_HEADS
    return pl.pallas_call(
        _peer_select_kernel,
        grid=(n // tn, PEER_HEADS),
        in_specs=[pl.BlockSpec((tn, qd), lambda i, h: (i, h)),
                  pl.BlockSpec((PEER_NKEYS, qd), lambda i, h: (0, 0)),
                  pl.BlockSpec((PEER_NKEYS, qd), lambda i, h: (0, 0))],
        out_specs=[pl.BlockSpec((1, PEER_NKEYS, tn), lambda i, h: (h, 0, i)),
                   pl.BlockSpec((1, PEER_NKEYS, tn), lambda i, h: (h, 0, i)),
                   pl.BlockSpec((1, 8, tn), lambda i, h: (h, 0, i))],
        out_shape=[jax.ShapeDtypeStruct((PEER_HEADS, PEER_NKEYS, n), F32),
                   jax.ShapeDtypeStruct((PEER_HEADS, PEER_NKEYS, n), F32),
                   jax.ShapeDtypeStruct((PEER_HEADS, 8, n), F32)],
        scratch_shapes=[pltpu.VMEM((PEER_NKEYS, tn), F32),
                        pltpu.VMEM((PEER_TOPK, tn), F32),
                        pltpu.VMEM((PEER_TOPK, tn), F32),
                        pltpu.VMEM((PEER_TOPK, tn), F32)],
        compiler_params=_cparams(2),
        name="peer_select",
    )(q, k1p, k2p)


def _peer_dense_kernel(h_ref, u_ref, vt_ref, s1_ref, s2_ref, stat_ref, x1_ref, y_ref,
                       acc_ref, e1_ref, e2_ref, w_ref):
    j = pl.program_id(1)
    nj = pl.num_programs(1)
    eb = u_ref.shape[0]
    ne1 = eb // PEER_NKEYS

    @pl.when(j == 0)
    def _():
        acc_ref[...] = jnp.zeros(acc_ref.shape, F32)
        for h in range(PEER_HEADS):
            st = stat_ref[h]
            e1_ref[h] = jnp.exp(s1_ref[h] - st[1:2, :]) * st[3:4, :]
            e2_ref[h] = jnp.exp(s2_ref[h] - st[2:3, :])

    act = lax.dot_general(u_ref[...], h_ref[...], (((1,), (1,)), ((), ())),
                          preferred_element_type=F32)
    for e in range(ne1):
        e1 = j * ne1 + e
        gate = jnp.zeros((PEER_NKEYS, act.shape[1]), F32)
        for h in range(PEER_HEADS):
            s1row = s1_ref[h, pl.ds(e1, 1), :]
            e1row = e1_ref[h, pl.ds(e1, 1), :]
            tau = stat_ref[h, 0:1, :]
            chosen = (s1row + s2_ref[h]) >= tau
            gate = gate + jnp.where(chosen, e2_ref[h] * e1row, 0.0)
        a = act[e * PEER_NKEYS:(e + 1) * PEER_NKEYS, :]
        gelu = 0.5 * a * (1.0 + lax.erf(a * (2.0 ** -0.5)))
        w_ref[e * PEER_NKEYS:(e + 1) * PEER_NKEYS, :] = (gate * gelu).astype(BF16)
    acc_ref[...] += jnp.dot(vt_ref[...], w_ref[...], preferred_element_type=F32)

    @pl.when(j == nj - 1)
    def _():
        y_ref[...] = x1_ref[...] + acc_ref[...].T


def _peer_dense(h, u, vt, s1, s2, stat, x1, tn, eb):
    n, d = x1.shape
    ne = u.shape[0]
    return pl.pallas_call(
        _peer_dense_kernel,
        grid=(n // tn, ne // eb),
        in_specs=[pl.BlockSpec((tn, d), lambda i, j: (i, 0)),
                  pl.BlockSpec((eb, d), lambda i, j: (j, 0)),
                  pl.BlockSpec((d, eb), lambda i, j: (0, j)),
                  pl.BlockSpec((PEER_HEADS, PEER_NKEYS, tn), lambda i, j: (0, 0, i)),
                  pl.BlockSpec((PEER_HEADS, PEER_NKEYS, tn), lambda i, j: (0, 0, i)),
                  pl.BlockSpec((PEER_HEADS, 8, tn), lambda i, j: (0, 0, i)),
                  pl.BlockSpec((tn, d), lambda i, j: (i, 0))],
        out_specs=[pl.BlockSpec((tn, d), lambda i, j: (i, 0))],
        out_shape=[jax.ShapeDtypeStruct((n, d), F32)],
        scratch_shapes=[pltpu.VMEM((d, tn), F32),
                        pltpu.VMEM((PEER_HEADS, PEER_NKEYS, tn), F32),
                        pltpu.VMEM((PEER_HEADS, PEER_NKEYS, tn), F32),
                        pltpu.VMEM((eb, tn), BF16)],
        compiler_params=_cparams(2),
        name="peer_dense",
    )(h, u, vt, s1, s2, stat, x1)[0]


def _rope_tables(pos):
    half = HEAD_DIM // 2
    inv_freq = ROPE_THETA ** (-jnp.arange(half, dtype=F32) / half)
    ang = pos.astype(F32)[:, None] * inv_freq[None, :]
    cos = jnp.tile(jnp.cos(ang), (1, 2 * N_HEADS))
    sin = jnp.sin(ang)
    sin_signed = jnp.tile(jnp.concatenate([-sin, sin], axis=1), (1, N_HEADS))
    return cos, sin_signed


def _row_tile(n, pref):
    return pref if n % pref == 0 else n


def _layer_params(l, norm_mix, w_in, q_norm, k_norm, conf_dw, conf_dw_b, conf_ln_g, conf_ln_b,
                  rwkv_mu, rwkv_w0, rwkv_w2, rwkv_a0, rwkv_a2, rwkv_g2, rwkv_kk, rwkv_ka, rwkv_rk,
                  rwkv_ln_g, rwkv_ln_b, sc_w, w_out, norm_ffn, peer_wq, peer_k1, peer_k2, peer_u, peer_v):
    row = lambda a: a[l].reshape(1, -1)
    att_c = 3 * GROUP
    conf_c = 2 * GROUP
    rw_c = rwkv_mu.shape[1]
    wi = w_in[l].astype(BF16)
    bounds = [0, att_c, att_c + conf_c, att_c + conf_c + rw_c, wi.shape[1]]
    rd, ra, rg = rwkv_w2.shape[1], rwkv_a2.shape[1], rwkv_g2.shape[1]
    zeros = lambda r: jnp.zeros((r, GROUP), F32)
    half = peer_k1.shape[2]
    return dict(
        norm_mix=row(norm_mix),
        w_in=[wi[:, bounds[i]:bounds[i + 1]] for i in range(4)],
        qg=jnp.tile(row(q_norm), (1, N_HEADS)), kg=jnp.tile(row(k_norm), (1, N_HEADS)),
        dw=conf_dw[l], dwb=row(conf_dw_b), clng=row(conf_ln_g), clnb=row(conf_ln_b), scw=sc_w[l],
        rwkv=dict(mu=row(rwkv_mu), w0=row(rwkv_w0), a0=row(rwkv_a0), kkw=row(rwkv_kk), ka=row(rwkv_ka),
                  rk=row(rwkv_rk), lng=row(rwkv_ln_g), lnb=row(rwkv_ln_b),
                  w2=jnp.concatenate([rwkv_w2[l], zeros(ra + rg)], axis=0),
                  a2=jnp.concatenate([zeros(rd), rwkv_a2[l], zeros(rg)], axis=0),
                  g2=jnp.concatenate([zeros(rd + ra), rwkv_g2[l]], axis=0)),
        w_out=w_out[l].astype(BF16).reshape(4, GROUP, -1),
        norm_ffn=row(norm_ffn), wq=peer_wq[l].astype(BF16),
        k1p=jnp.concatenate([peer_k1[l], jnp.zeros_like(peer_k1[l])], axis=1).astype(BF16),
        k2p=jnp.concatenate([jnp.zeros_like(peer_k2[l]), peer_k2[l]], axis=1).astype(BF16),
        u=peer_u[l].astype(BF16), vt=peer_v[l].astype(BF16).T,
    )


def _mix_and_ffn(x, oa, ob, oc, od, p):
    b, t, d = x.shape
    n = b * t
    flat = lambda a: a.reshape(n, a.shape[-1])
    tm = _row_tile(n, 512)
    x1, h2, q = _outproj(flat(oa), flat(ob), flat(oc), flat(od), flat(x), p["w_out"], p["norm_ffn"],
                         p["wq"], tm)
    tn = _row_tile(n, 512)
    s1, s2, stat = _peer_select(q, p["k1p"], p["k2p"], tn)
    y = _peer_dense(h2, p["u"], p["vt"], s1, s2, stat, x1, tn, 512)
    return y.reshape(b, t, d)


def _common_mixers(x, p, conf_buf, shift_prev, wkv_prev, sc_buf, chunk, conv_tile):
    b, t, d = x.shape
    n = b * t
    att, conf, rw, sc = _proj(x.reshape(n, d), p["norm_mix"], p["w_in"], _row_tile(n, 512))
    att = att.reshape(b, t, -1)
    conf = conf.reshape(b, t, -1)
    rw = rw.reshape(b, t, -1)
    sc = sc.reshape(b, t, -1)
    ob, od, conf_new, sc_new = _conv(conf, sc, conf_buf, sc_buf, p["dw"], p["dwb"], p["clng"], p["clnb"],
                                     p["scw"], conv_tile)
    oc, st_out = _rwkv(rw, shift_prev[:, None, :], _state_to_blockdiag(wkv_prev), p["rwkv"], chunk)
    return att, ob, oc, od, conf_new, rw[:, -1, :], _blockdiag_to_state(st_out), sc_new


def kernel(x_prompt, x_sample, cache_k, cache_v, page_table, state_conf_conv, state_rwkv_shift, state_rwkv_wkv, state_short_conv, norm_mix, w_in, q_norm, k_norm, conf_dw, conf_dw_b, conf_ln_g, conf_ln_b, rwkv_mu, rwkv_w0, rwkv_w2, rwkv_a0, rwkv_a2, rwkv_g2, rwkv_kk, rwkv_ka, rwkv_rk, rwkv_ln_g, rwkv_ln_b, sc_w, w_out, norm_ffn, peer_wq, peer_k1, peer_k2, peer_u, peer_v):
    depth = w_in.shape[0]
    bp, seq, _ = x_prompt.shape
    db, dec_seq, _ = x_sample.shape
    n_pool = cache_k.shape[1]
    past_len = page_table.shape[1] * PAGE
    ck = cache_k.reshape(depth * n_pool, PAGE, GROUP)
    cv = cache_v.reshape(depth * n_pool, PAGE, GROUP)
    cos_p, sin_p = _rope_tables(jnp.arange(seq))
    cos_s, sin_s = _rope_tables(past_len + jnp.arange(dec_seq))
    dt = x_prompt.dtype

    yp, ys = x_prompt, x_sample
    outs_p = [[] for _ in range(6)]
    outs_s = [[] for _ in range(6)]
    for l in range(depth):
        p = _layer_params(l, norm_mix, w_in, q_norm, k_norm, conf_dw, conf_dw_b, conf_ln_g, conf_ln_b,
                          rwkv_mu, rwkv_w0, rwkv_w2, rwkv_a0, rwkv_a2, rwkv_g2, rwkv_kk, rwkv_ka,
                          rwkv_rk, rwkv_ln_g, rwkv_ln_b, sc_w, w_out, norm_ffn, peer_wq, peer_k1,
                          peer_k2, peer_u, peer_v)
        att, ob, oc, od, conf_new, shift_new, wkv_new, sc_new = _common_mixers(
            yp, p,
            jnp.zeros((bp, CONF_TAPS - 1, GROUP), dt),
            jnp.zeros((bp, rwkv_mu.shape[1]), dt),
            jnp.zeros((bp, N_HEADS, HEAD_DIM, HEAD_DIM), dt),
            jnp.zeros((bp, SC_TAPS - 1, GROUP), dt),
            chunk=64, conv_tile=_row_tile(seq, 512))
        oa, k_rot = _moba_prompt(att, cos_p, sin_p, p["qg"], p["kg"])
        new_p = (k_rot.reshape(bp, seq, N_HEADS, HEAD_DIM),
                 att[:, :, 2 * GROUP:].reshape(bp, seq, N_HEADS, HEAD_DIM),
                 conf_new, shift_new, wkv_new, sc_new)
        yp = _mix_and_ffn(yp, oa, ob, oc, od, p)

        att, ob, oc, od, conf_new, shift_new, wkv_new, sc_new = _common_mixers(
            ys, p, state_conf_conv[l], state_rwkv_shift[l], state_rwkv_wkv[l], state_short_conv[l],
            chunk=dec_seq, conv_tile=dec_seq)
        sel, q_rot, k_rot = _decode_sel(att, cos_s, sin_s, p["qg"], p["kg"], ck, page_table, l * n_pool)
        v_new = att[:, :, 2 * GROUP:]
        oa = _decode_att(sel, q_rot, k_rot, v_new, ck, cv, page_table, l * n_pool)
        new_s = (k_rot.reshape(db, dec_seq, N_HEADS, HEAD_DIM),
                 v_new.reshape(db, dec_seq, N_HEADS, HEAD_DIM),
                 conf_new, shift_new, wkv_new, sc_new)
        ys = _mix_and_ffn(ys, oa, ob, oc, od, p)

        for lst, a in zip(outs_p, new_p):
            lst.append(a)
        for lst, a in zip(outs_s, new_s):
            lst.append(a)
    pk, pv, pconf, pshift, pwkv, psc = [jnp.stack(a) for a in outs_p]
    sk, sv, sconf, sshift, swkv, ssc = [jnp.stack(a) for a in outs_s]
    return (yp, ys, pk, pv, sk, sv, pconf, sconf, pshift, sshift, pwkv, swkv, psc, ssc)
```

```python
import functools
import math

import jax
import jax.numpy as jnp
from jax import lax
from jax.experimental import pallas as pl
from jax.experimental.pallas import tpu as pltpu

F32 = jnp.float32
BF16 = jnp.bfloat16

HEAD_DIM = 64
GROUP = 256
N_HEADS = GROUP // HEAD_DIM
MOBA_BLOCK = 256
MOBA_TOPK = 3
ROPE_THETA = 10000.0
PAGE = 128
CONF_TAPS = 31
SC_TAPS = 3
RWKV_LOW = 128
PEER_HEADS = 8
PEER_NKEYS = 128
PEER_TOPK = 16
RMS_EPS = 1e-6
LN_EPS = 1e-5
RWKV_GN_EPS = 64e-5
NEG_INF = -1e30
LANES = 128
VMEM_LIMIT = 56 * 1024 * 1024

_ARB = "arbitrary"


def _cparams(n_axes, vmem=VMEM_LIMIT):
    return pltpu.CompilerParams(dimension_semantics=(_ARB,) * n_axes, vmem_limit_bytes=vmem)


def _mm(a, b):
    return jnp.dot(a.astype(BF16), b.astype(BF16), preferred_element_type=F32)


def _mm_nt(a, b):
    return lax.dot_general(a.astype(BF16), b.astype(BF16), (((1,), (1,)), ((), ())),
                           preferred_element_type=F32)


def _split(a):
    hi = a.astype(BF16)
    lo = (a - hi.astype(F32)).astype(BF16)
    return hi, lo


def _mm3(a, b):
    ah, al = _split(a)
    bh, bl = _split(b)
    d = functools.partial(jnp.dot, preferred_element_type=F32)
    return d(ah, bh) + d(ah, bl) + d(al, bh)


def _mm3_nt(a, b):
    ah, al = _split(a)
    bh, bl = _split(b)
    d = functools.partial(lax.dot_general, dimension_numbers=(((1,), (1,)), ((), ())),
                          preferred_element_type=F32)
    return d(ah, bh) + d(ah, bl) + d(al, bh)


def _mm2_exact_rhs(a, b_bf16):
    ah, al = _split(a)
    d = functools.partial(jnp.dot, preferred_element_type=F32)
    return d(ah, b_bf16) + d(al, b_bf16)


def _sigmoid(x):
    return 1.0 / (1.0 + jnp.exp(-x))


def _head_of_lane(shape, axis):
    return lax.broadcasted_iota(jnp.int32, shape, axis) // HEAD_DIM


def _head_sum_matrix():
    r = _head_of_lane((GROUP, GROUP), 0)
    c = _head_of_lane((GROUP, GROUP), 1)
    return jnp.where(r == c, 1.0, 0.0).astype(BF16)


def _proj_kernel(x_ref, g_ref, wa_ref, wc_ref, wr_ref, ws_ref, oa_ref, oc_ref, or_ref, os_ref):
    x = x_ref[...]
    ms = jnp.mean(x * x, axis=-1, keepdims=True)
    h = (x * lax.rsqrt(ms + RMS_EPS) * g_ref[...]).astype(BF16)
    for w_ref, o_ref in ((wa_ref, oa_ref), (wc_ref, oc_ref), (wr_ref, or_ref), (ws_ref, os_ref)):
        o_ref[...] = jnp.dot(h, w_ref[...], preferred_element_type=F32)


def _proj(x, g, ws, tm):
    n, d = x.shape
    widths = [w.shape[1] for w in ws]
    return pl.pallas_call(
        _proj_kernel,
        grid=(n // tm,),
        in_specs=[pl.BlockSpec((tm, d), lambda i: (i, 0)),
                  pl.BlockSpec((1, d), lambda i: (0, 0))]
                 + [pl.BlockSpec((d, wd), lambda i: (0, 0)) for wd in widths],
        out_specs=[pl.BlockSpec((tm, wd), lambda i: (i, 0)) for wd in widths],
        out_shape=[jax.ShapeDtypeStruct((n, wd), F32) for wd in widths],
        compiler_params=_cparams(1),
        name="proj",
    )(x, g, *ws)


def _head_rms(x, gain, bd):
    ms = _mm2_exact_rhs(x * x, bd) * (1.0 / HEAD_DIM)
    return x * lax.rsqrt(ms + RMS_EPS) * gain


def _rope(x, cos, sin_signed):
    lane = lax.broadcasted_iota(jnp.int32, x.shape, 1) % HEAD_DIM
    lo = pltpu.roll(x, GROUP - HEAD_DIM // 2, axis=1)
    hi = pltpu.roll(x, HEAD_DIM // 2, axis=1)
    rot = jnp.where(lane < HEAD_DIM // 2, lo, hi)
    return x * cos + rot * sin_signed


def _top_rows(g, n_rows, k):
    ridx = lax.broadcasted_iota(jnp.int32, g.shape, 0).astype(F32)
    sel = jnp.zeros(g.shape, F32)
    cur = g
    for _ in range(k):
        mx = jnp.max(cur, axis=0, keepdims=True)
        first = jnp.min(jnp.where(cur == mx, ridx, float(n_rows)), axis=0, keepdims=True)
        hit = ridx == first
        sel = jnp.where(hit, 1.0, sel)
        cur = jnp.where(hit, -jnp.inf, cur)
    return sel


def _moba_prompt_kernel(att_ref, cos_ref, sin_ref, qg_ref, kg_ref, oa_ref, kr_ref,
                        kb_ref, vt_ref, km_ref, sel_ref, qm_ref):
    qi = pl.program_id(1)
    nblk = kb_ref.shape[0]
    blk = MOBA_BLOCK
    bd = _head_sum_matrix()

    @pl.when(qi == 0)
    def _():
        km_ref[...] = jnp.zeros(km_ref.shape, F32)

    a = att_ref[0]
    cos = cos_ref[...]
    sin = sin_ref[...]
    qr = _rope(_head_rms(a[:, 0:GROUP], qg_ref[...], bd), cos, sin)
    kr = _rope(_head_rms(a[:, GROUP:2 * GROUP], kg_ref[...], bd), cos, sin)
    v = a[:, 2 * GROUP:3 * GROUP]
    kr_ref[0] = kr
    kb_ref[qi] = kr.astype(BF16)
    vt_ref[qi] = v.T.astype(BF16)
    km_ref[pl.ds(qi, 1), :] = jnp.mean(kr, axis=0, keepdims=True)

    qs = qr * (HEAD_DIM ** -0.5)
    lane_head = _head_of_lane((blk, GROUP), 1)
    km = km_ref[...]
    km_head = _head_of_lane(km.shape, 1)
    blk_row = lax.broadcasted_iota(jnp.int32, (nblk, blk), 0)
    krow = lax.broadcasted_iota(jnp.int32, (blk, blk), 0)
    qcol = lax.broadcasted_iota(jnp.int32, (blk, blk), 1)
    kcur = kb_ref[qi]
    heads = range(N_HEADS)
    head_rows = [slice(h * HEAD_DIM, (h + 1) * HEAD_DIM) for h in heads]
    nt = functools.partial(lax.dot_general, dimension_numbers=(((1,), (1,)), ((), ())),
                           preferred_element_type=F32)
    qms = [jnp.where(lane_head == h, qs, 0.0) for h in heads]
    for h in heads:
        qm_ref[h] = qms[h].astype(BF16)
    past = blk_row < qi
    gates = [_mm3_nt(jnp.where(km_head == h, km, 0.0), qms[h]) for h in heads]
    for h in heads:
        sel = _top_rows(jnp.where(past, gates[h], NEG_INF), nblk, MOBA_TOPK)
        sel_ref[h] = jnp.where(past, sel, 0.0)
    sts = [jnp.where(krow <= qcol, nt(kcur, qm_ref[h]), NEG_INF) for h in heads]
    ms = [jnp.max(st, axis=0, keepdims=True) for st in sts]
    ps = [jnp.exp(st - m) for st, m in zip(sts, ms)]
    ls = [jnp.sum(p, axis=0, keepdims=True) for p in ps]
    accs = [jnp.dot(vt_ref[qi, head_rows[h], :], ps[h].astype(BF16), preferred_element_type=F32)
            for h in heads]

    def past_block(j, carry):
        ms, ls, accs = carry
        kj = kb_ref[j]
        sts = [jnp.where(sel_ref[h, pl.ds(j, 1), :] > 0.0, nt(kj, qm_ref[h]), NEG_INF) for h in heads]
        m_new = [jnp.maximum(m, jnp.max(st, axis=0, keepdims=True)) for m, st in zip(ms, sts)]
        alpha = [jnp.exp(m - mn) for m, mn in zip(ms, m_new)]
        ps = [jnp.exp(st - mn) for st, mn in zip(sts, m_new)]
        ls = [a * l + jnp.sum(p, axis=0, keepdims=True) for a, l, p in zip(alpha, ls, ps)]
        accs = [alpha[h] * accs[h] + jnp.dot(vt_ref[j, head_rows[h], :], ps[h].astype(BF16),
                                             preferred_element_type=F32) for h in heads]
        return m_new, ls, accs

    ms, ls, accs = lax.fori_loop(0, qi, past_block, (ms, ls, accs))
    oa_ref[0] = jnp.concatenate([acc / l for acc, l in zip(accs, ls)], axis=0).T


def _moba_prompt(att, cos, sin, qg, kg):
    b, t, _ = att.shape
    nblk = t // MOBA_BLOCK
    blk = MOBA_BLOCK
    return pl.pallas_call(
        _moba_prompt_kernel,
        grid=(b, nblk),
        in_specs=[pl.BlockSpec((1, blk, 3 * GROUP), lambda i, j: (i, j, 0)),
                  pl.BlockSpec((blk, GROUP), lambda i, j: (j, 0)),
                  pl.BlockSpec((blk, GROUP), lambda i, j: (j, 0)),
                  pl.BlockSpec((1, GROUP), lambda i, j: (0, 0)),
                  pl.BlockSpec((1, GROUP), lambda i, j: (0, 0))],
        out_specs=[pl.BlockSpec((1, blk, GROUP), lambda i, j: (i, j, 0)),
                   pl.BlockSpec((1, blk, GROUP), lambda i, j: (i, j, 0))],
        out_shape=[jax.ShapeDtypeStruct((b, t, GROUP), F32),
                   jax.ShapeDtypeStruct((b, t, GROUP), F32)],
        scratch_shapes=[pltpu.VMEM((nblk, blk, GROUP), BF16),
                        pltpu.VMEM((nblk, GROUP, blk), BF16),
                        pltpu.VMEM((nblk, GROUP), F32),
                        pltpu.VMEM((N_HEADS, nblk, blk), F32),
                        pltpu.VMEM((N_HEADS, blk, GROUP), BF16)],
        compiler_params=_cparams(2),
        name="moba_prompt",
    )(att, cos, sin, qg, kg)


def _decode_sel_kernel(pt_ref, att_ref, cos_ref, sin_ref, qg_ref, kg_ref, *rest, pages_per_step):
    page_refs = rest[:pages_per_step]
    sel_ref, qr_ref, kr_ref, km_ref = rest[pages_per_step:]
    s = pl.program_id(1)
    n_steps = pl.num_programs(1)
    blocks_per_step = pages_per_step * PAGE // MOBA_BLOCK
    n_past = n_steps * blocks_per_step
    pages_per_block = MOBA_BLOCK // PAGE
    bd = _head_sum_matrix()

    @pl.when(s == 0)
    def _():
        km_ref[...] = jnp.zeros(km_ref.shape, F32)
        a = att_ref[0]
        qr_ref[0] = _rope(_head_rms(a[:, 0:GROUP], qg_ref[...], bd), cos_ref[...], sin_ref[...])
        kr_ref[0] = _rope(_head_rms(a[:, GROUP:2 * GROUP], kg_ref[...], bd), cos_ref[...], sin_ref[...])

    for bi in range(blocks_per_step):
        tot = jnp.zeros((1, GROUP), F32)
        for pi in range(pages_per_block):
            tot = tot + jnp.sum(page_refs[bi * pages_per_block + pi][0], axis=0, keepdims=True)
        km_ref[pl.ds(s * blocks_per_step + bi, 1), :] = tot * (1.0 / MOBA_BLOCK)

    @pl.when(s == n_steps - 1)
    def _():
        qs = qr_ref[0] * (HEAD_DIM ** -0.5)
        km = km_ref[...]
        lane_head = _head_of_lane(qs.shape, 1)
        gates = []
        for h in range(N_HEADS):
            gates.append(_mm3_nt(jnp.where(lane_head == h, qs, 0.0), km))
        gate = jnp.concatenate(gates, axis=0)
        lane = lax.broadcasted_iota(jnp.int32, gate.shape, 1)
        past = lane < n_past
        cur = jnp.where(past, gate, NEG_INF)
        lidx = lane.astype(F32)
        sel = jnp.zeros(gate.shape, F32)
        for _ in range(MOBA_TOPK):
            mx = jnp.max(cur, axis=1, keepdims=True)
            first = jnp.min(jnp.where(cur == mx, lidx, float(LANES)), axis=1, keepdims=True)
            hit = lidx == first
            sel = jnp.where(hit, 1.0, sel)
            cur = jnp.where(hit, -jnp.inf, cur)
        sel_ref[0] = jnp.where(past, sel, 0.0)


def _decode_sel(att, cos, sin, qg, kg, cache_k, page_table, layer_off, pages_per_step=8):
    db, t, _ = att.shape
    n_pages = page_table.shape[1]
    n_steps = n_pages // pages_per_step
    assert n_pages * PAGE // MOBA_BLOCK <= LANES

    def page_map(i):
        return lambda b, s, pt: (pt[b, s * pages_per_step + i] + layer_off, 0, 0)

    grid_spec = pltpu.PrefetchScalarGridSpec(
        num_scalar_prefetch=1,
        grid=(db, n_steps),
        in_specs=[pl.BlockSpec((1, t, 3 * GROUP), lambda b, s, pt: (b, 0, 0)),
                  pl.BlockSpec((t, GROUP), lambda b, s, pt: (0, 0)),
                  pl.BlockSpec((t, GROUP), lambda b, s, pt: (0, 0)),
                  pl.BlockSpec((1, GROUP), lambda b, s, pt: (0, 0)),
                  pl.BlockSpec((1, GROUP), lambda b, s, pt: (0, 0))]
                 + [pl.BlockSpec((1, PAGE, GROUP), page_map(i)) for i in range(pages_per_step)],
        out_specs=[pl.BlockSpec((1, N_HEADS * t, LANES), lambda b, s, pt: (b, 0, 0)),
                   pl.BlockSpec((1, t, GROUP), lambda b, s, pt: (b, 0, 0)),
                   pl.BlockSpec((1, t, GROUP), lambda b, s, pt: (b, 0, 0))],
        scratch_shapes=[pltpu.VMEM((LANES, GROUP), F32)],
    )
    return pl.pallas_call(
        functools.partial(_decode_sel_kernel, pages_per_step=pages_per_step),
        grid_spec=grid_spec,
        out_shape=[jax.ShapeDtypeStruct((db, N_HEADS * t, LANES), F32),
                   jax.ShapeDtypeStruct((db, t, GROUP), F32),
                   jax.ShapeDtypeStruct((db, t, GROUP), F32)],
        compiler_params=_cparams(2),
        name="decode_sel",
    )(page_table, att, cos, sin, qg, kg, *([cache_k] * pages_per_step))


def _decode_att_kernel(pt_ref, sel_ref, qr_ref, kn_ref, vn_ref, *rest, pages_per_step):
    k_refs = rest[:pages_per_step]
    v_refs = rest[pages_per_step:2 * pages_per_step]
    oa_ref, qbd_ref, acc_ref, m_ref, l_ref = rest[2 * pages_per_step:]
    j = pl.program_id(1)
    n_steps = pl.num_programs(1)
    t = qr_ref.shape[1]
    blocks_per_step = pages_per_step * PAGE // MOBA_BLOCK

    def online_update(st, vb):
        m_old = m_ref[...]
        m_new = jnp.maximum(m_old, jnp.max(st, axis=1, keepdims=True))
        alpha = jnp.exp(m_old - m_new)
        p = jnp.exp(st - m_new)
        m_ref[...] = m_new
        l_ref[...] = alpha * l_ref[...] + jnp.sum(p, axis=1, keepdims=True)
        acc_ref[...] = alpha * acc_ref[...] + jnp.dot(p.astype(BF16), vb, preferred_element_type=F32)

    @pl.when(j == 0)
    def _():
        qs = qr_ref[0] * (HEAD_DIM ** -0.5)
        lane_head = _head_of_lane(qs.shape, 1)
        qbd = jnp.concatenate([jnp.where(lane_head == h, qs, 0.0) for h in range(N_HEADS)], axis=0)
        qbd_ref[...] = qbd.astype(BF16)
        pad = jnp.zeros((MOBA_BLOCK - t, GROUP), F32)
        kn = jnp.concatenate([kn_ref[0], pad], axis=0).astype(BF16)
        vn = jnp.concatenate([vn_ref[0], pad], axis=0).astype(BF16)
        st = lax.dot_general(qbd_ref[...], kn, (((1,), (1,)), ((), ())), preferred_element_type=F32)
        qpos = lax.broadcasted_iota(jnp.int32, st.shape, 0) % t
        kpos = lax.broadcasted_iota(jnp.int32, st.shape, 1)
        st = jnp.where(kpos <= qpos, st, NEG_INF)
        m = jnp.max(st, axis=1, keepdims=True)
        p = jnp.exp(st - m)
        m_ref[...] = m
        l_ref[...] = jnp.sum(p, axis=1, keepdims=True)
        acc_ref[...] = jnp.dot(p.astype(BF16), vn, preferred_element_type=F32)

    kb = jnp.concatenate([r[0] for r in k_refs], axis=0).astype(BF16)
    vb = jnp.concatenate([r[0] for r in v_refs], axis=0).astype(BF16)
    st = lax.dot_general(qbd_ref[...], kb, (((1,), (1,)), ((), ())), preferred_element_type=F32)
    sel = sel_ref[0]
    lane = lax.broadcasted_iota(jnp.int32, sel.shape, 1)
    masked = []
    for bi in range(blocks_per_step):
        chosen = jnp.sum(jnp.where(lane == j * blocks_per_step + bi, sel, 0.0), axis=1, keepdims=True) > 0.0
        masked.append(jnp.where(chosen, st[:, bi * MOBA_BLOCK:(bi + 1) * MOBA_BLOCK], NEG_INF))
    online_update(jnp.concatenate(masked, axis=1), vb)

    @pl.when(j == n_steps - 1)
    def _():
        o = acc_ref[...] / l_ref[...]
        lane_head = _head_of_lane((t, GROUP), 1)
        out = jnp.zeros((t, GROUP), F32)
        for h in range(N_HEADS):
            out = out + jnp.where(lane_head == h, o[h * t:(h + 1) * t, :], 0.0)
        oa_ref[0] = out


def _decode_att(sel, qr, kn, vn, cache_k, cache_v, page_table, layer_off, pages_per_step=8):
    db, t, _ = qr.shape
    n_steps = page_table.shape[1] // pages_per_step
    rows = N_HEADS * t

    def page_map(i):
        return lambda b, j, pt: (pt[b, pages_per_step * j + i] + layer_off, 0, 0)

    pages = [pl.BlockSpec((1, PAGE, GROUP), page_map(i)) for i in range(pages_per_step)]
    grid_spec = pltpu.PrefetchScalarGridSpec(
        num_scalar_prefetch=1,
        grid=(db, n_steps),
        in_specs=[pl.BlockSpec((1, rows, LANES), lambda b, j, pt: (b, 0, 0)),
                  pl.BlockSpec((1, t, GROUP), lambda b, j, pt: (b, 0, 0)),
                  pl.BlockSpec((1, t, GROUP), lambda b, j, pt: (b, 0, 0)),
                  pl.BlockSpec((1, t, GROUP), lambda b, j, pt: (b, 0, 0))] + pages + pages,
        out_specs=[pl.BlockSpec((1, t, GROUP), lambda b, j, pt: (b, 0, 0))],
        scratch_shapes=[pltpu.VMEM((rows, GROUP), BF16),
                        pltpu.VMEM((rows, GROUP), F32),
                        pltpu.VMEM((rows, 1), F32),
                        pltpu.VMEM((rows, 1), F32)],
    )
    return pl.pallas_call(
        functools.partial(_decode_att_kernel, pages_per_step=pages_per_step),
        grid_spec=grid_spec,
        out_shape=[jax.ShapeDtypeStruct((db, t, GROUP), F32)],
        compiler_params=_cparams(2),
        name="decode_att",
    )(page_table, sel, qr, kn, vn, *([cache_k] * pages_per_step), *([cache_v] * pages_per_step))[0]


def _conv_kernel(conf_ref, sc_ref, cbuf_ref, sbuf_ref, dw_ref, dwb_ref, lng_ref, lnb_ref, scw_ref,
                 ob_ref, od_ref, cnew_ref, snew_ref, extc_ref, exts_ref):
    ti = pl.program_id(1)
    nt = pl.num_programs(1)
    tt = conf_ref.shape[1]
    hc = 32
    hs = 8
    nbc = CONF_TAPS - 1
    nbs = SC_TAPS - 1

    @pl.when(ti == 0)
    def _():
        extc_ref[0:hc - nbc, :] = jnp.zeros((hc - nbc, GROUP), F32)
        extc_ref[hc - nbc:hc, :] = cbuf_ref[0]
        exts_ref[0:hs - nbs, :] = jnp.zeros((hs - nbs, GROUP), F32)
        exts_ref[hs - nbs:hs, :] = sbuf_ref[0]

    c = conf_ref[0]
    glu = c[:, 0:GROUP] * _sigmoid(c[:, GROUP:2 * GROUP])
    extc_ref[hc:hc + tt, :] = glu
    acc = jnp.zeros((tt, GROUP), F32) + dwb_ref[...]
    for j in range(CONF_TAPS):
        acc = acc + dw_ref[j:j + 1, :] * extc_ref[pl.ds(hc - nbc + j, tt), :]
    mu = jnp.mean(acc, axis=-1, keepdims=True)
    var = jnp.mean(jnp.square(acc - mu), axis=-1, keepdims=True)
    ln = (acc - mu) * lax.rsqrt(var + LN_EPS) * lng_ref[...] + lnb_ref[...]
    ob_ref[0] = ln * _sigmoid(ln)

    s = sc_ref[0]
    gated = s[:, GROUP:2 * GROUP] * s[:, 2 * GROUP:3 * GROUP]
    exts_ref[hs:hs + tt, :] = gated
    d = jnp.zeros((tt, GROUP), F32)
    for j in range(SC_TAPS):
        d = d + scw_ref[j:j + 1, :] * exts_ref[pl.ds(hs - nbs + j, tt), :]
    od_ref[0] = s[:, 0:GROUP] * d

    @pl.when(ti == nt - 1)
    def _():
        cnew_ref[0] = extc_ref[pl.ds(tt + hc - nbc, nbc), :]
        snew_ref[0] = exts_ref[pl.ds(tt + hs - nbs, nbs), :]

    tail_c = extc_ref[pl.ds(tt, hc), :]
    tail_s = exts_ref[pl.ds(tt, hs), :]
    extc_ref[0:hc, :] = tail_c
    exts_ref[0:hs, :] = tail_s


def _conv(conf, sc, cbuf, sbuf, dw, dwb, lng, lnb, scw, tt):
    b, t, _ = conf.shape
    nbc = CONF_TAPS - 1
    nbs = SC_TAPS - 1
    const = lambda i, j: (0, 0)
    return pl.pallas_call(
        _conv_kernel,
        grid=(b, t // tt),
        in_specs=[pl.BlockSpec((1, tt, 2 * GROUP), lambda i, j: (i, j, 0)),
                  pl.BlockSpec((1, tt, 3 * GROUP), lambda i, j: (i, j, 0)),
                  pl.BlockSpec((1, nbc, GROUP), lambda i, j: (i, 0, 0)),
                  pl.BlockSpec((1, nbs, GROUP), lambda i, j: (i, 0, 0)),
                  pl.BlockSpec((CONF_TAPS, GROUP), const),
                  pl.BlockSpec((1, GROUP), const),
                  pl.BlockSpec((1, GROUP), const),
                  pl.BlockSpec((1, GROUP), const),
                  pl.BlockSpec((SC_TAPS, GROUP), const)],
        out_specs=[pl.BlockSpec((1, tt, GROUP), lambda i, j: (i, j, 0)),
                   pl.BlockSpec((1, tt, GROUP), lambda i, j: (i, j, 0)),
                   pl.BlockSpec((1, nbc, GROUP), lambda i, j: (i, 0, 0)),
                   pl.BlockSpec((1, nbs, GROUP), lambda i, j: (i, 0, 0))],
        out_shape=[jax.ShapeDtypeStruct((b, t, GROUP), F32),
                   jax.ShapeDtypeStruct((b, t, GROUP), F32),
                   jax.ShapeDtypeStruct((b, nbc, GROUP), F32),
                   jax.ShapeDtypeStruct((b, nbs, GROUP), F32)],
        scratch_shapes=[pltpu.VMEM((32 + tt, GROUP), F32),
                        pltpu.VMEM((8 + tt, GROUP), F32)],
        compiler_params=_cparams(2),
        name="conv",
    )(conf, sc, cbuf, sbuf, dw, dwb, lng, lnb, scw)


def _rwkv_kernel(rw_ref, shift_ref, st0_ref, mu_ref, w0_ref, a0_ref, kkw_ref, ka_ref, rk_ref,
                 lng_ref, lnb_ref, w2_ref, a2_ref, g2_ref, oc_ref, sto_ref, st_ref, prev_ref):
    c = pl.program_id(1)

    @pl.when(c == 0)
    def _():
        st_ref[...] = st0_ref[...]
        prev_ref[...] = shift_ref[...]

    nseq, L, _ = rw_ref.shape
    SL = max(N_HEADS * L, LANES)
    seqs = range(nseq)
    each = lambda f, *cols: [f(*args) for args in zip(*cols)]
    P = [_rwkv_prep(s, rw_ref, mu_ref, w0_ref, a0_ref, kkw_ref, ka_ref, w2_ref, a2_ref, g2_ref, prev_ref)
         for s in seqs]
    at_s, bt_s, kt_s, rt_s, bg_s, kg_s, v_s = ([p[i] for p in P] for i in range(7))

    srow = lax.broadcasted_iota(jnp.int32, (SL, SL), 0)
    scol = lax.broadcasted_iota(jnp.int32, (SL, SL), 1)
    same = (srow // L) == (scol // L)
    strict = same & ((srow % L) > (scol % L))
    incl = same & ((srow % L) >= (scol % L))
    eye = jnp.where(srow == scol, 1.0, 0.0)
    n_ab = each(lambda a, b: jnp.where(strict, _mm_nt(a, b), 0.0), at_s, bt_s)
    n_ak = each(lambda a, b: jnp.where(strict, _mm_nt(a, b), 0.0), at_s, kt_s)
    q_rb = each(lambda a, b: jnp.where(incl, _mm_nt(a, b), 0.0), rt_s, bt_s)
    q_rk = each(lambda a, b: jnp.where(incl, _mm_nt(a, b), 0.0), rt_s, kt_s)

    tm = [eye + n for n in n_ab]
    pw = n_ab
    for _ in range(int(math.log2(L)) - 1):
        pw = each(_mm, pw, pw)
        tm = each(lambda t, x: t + _mm(t, x), tm, pw)

    p1 = each(_mm, tm, at_s)
    p2 = each(_mm, tm, each(_mm, n_ak, v_s))
    bg_t = [u.T for u in bg_s]
    kg_t = [u.T for u in kg_s]
    drow = lax.broadcasted_iota(jnp.int32, (GROUP, GROUP), 0)
    dcol = lax.broadcasted_iota(jnp.int32, (GROUP, GROUP), 1)
    m_mat = each(lambda p, bt, x: jnp.where(drow == dcol, jnp.exp(p[7]), 0.0) + _mm3(bt, x), P, bg_t, p1)
    c_mat = each(lambda bt, x, kt, vs: _mm3(bt, x) + _mm3(kt, vs), bg_t, p2, kg_t, v_s)
    yc = each(lambda rt, q, x: rt + _mm(q, x), rt_s, q_rb, p1)
    y0 = each(lambda q, x, qk, vs: _mm(q, x) + _mm(qk, vs), q_rb, p2, q_rk, v_s)

    st = [st_ref[s] for s in seqs]
    y_s = each(lambda a, b, y: _mm3(a, b) + y, yc, st, y0)
    st_new = each(lambda m, b, cm: _mm3(m, b) + cm, m_mat, st, c_mat)
    bd = _head_sum_matrix()
    inv_n = 1.0 / HEAD_DIM
    for s in seqs:
        st_ref[s] = st_new[s]
        y = y_s[s][0:L, :]
        for h in range(1, N_HEADS):
            y = y + y_s[s][h * L:(h + 1) * L, :]
        r, k2, v, g = P[s][8:12]
        mean = _mm2_exact_rhs(y, bd) * inv_n
        yc0 = y - mean
        var = _mm2_exact_rhs(yc0 * yc0, bd) * inv_n
        yn = yc0 * lax.rsqrt(var + RWKV_GN_EPS) * lng_ref[...] + lnb_ref[...]
        bonus = _mm2_exact_rhs(r * k2 * rk_ref[...], bd) * v
        oc_ref[s] = (yn + bonus) * g

    @pl.when(c == pl.num_programs(1) - 1)
    def _():
        sto_ref[...] = st_ref[...]


def _rwkv_prep(s, rw_ref, mu_ref, w0_ref, a0_ref, kkw_ref, ka_ref, w2_ref, a2_ref, g2_ref, prev_ref):
    L = rw_ref.shape[1]
    SL = max(N_HEADS * L, LANES)
    bd = _head_sum_matrix()

    z = rw_ref[s]
    row = lax.broadcasted_iota(jnp.int32, z.shape, 0)
    zprev = jnp.where(row == 0, prev_ref[s], pltpu.roll(z, 1, axis=0))
    prev_ref[s] = z[L - 1:L, :]
    zs = z + (zprev - z) * mu_ref[...]
    r = zs[:, 0:GROUP]
    k = zs[:, GROUP:2 * GROUP]
    v = zs[:, 2 * GROUP:3 * GROUP]
    low = zs[:, 3 * GROUP:3 * GROUP + RWKV_LOW]

    x = -(w0_ref[...] + _mm3(jnp.tanh(low), w2_ref[...]))
    softplus = jnp.maximum(x, 0.0) + jnp.log(1.0 + jnp.exp(-jnp.abs(x)))
    ld = -jnp.exp(-softplus - 0.5)
    a = _sigmoid(a0_ref[...] + _mm3(low, a2_ref[...]))
    g = _mm(_sigmoid(low), g2_ref[...])
    kk = k * kkw_ref[...]
    kk = kk / jnp.maximum(jnp.sqrt(_mm2_exact_rhs(kk * kk, bd)), 1e-12)
    k2 = k * (1.0 + (a - 1.0) * ka_ref[...])

    trow = lax.broadcasted_iota(jnp.int32, (L, L), 0)
    tcol = lax.broadcasted_iota(jnp.int32, (L, L), 1)
    tri = jnp.where(trow >= tcol, 1.0, 0.0).astype(BF16)
    cum = _mm2_exact_rhs_lhs(tri, ld)
    cend = cum[L - 1:L, :]
    g_prev = jnp.exp(cum - ld)
    g_inv = jnp.exp(-cum)
    g_rel = jnp.exp(cend - cum)
    kka = kk * a

    lane_head = _head_of_lane((L, GROUP), 1)

    def stack(u):
        parts = [jnp.where(lane_head == h, u, 0.0) for h in range(N_HEADS)]
        if SL > N_HEADS * L:
            parts.append(jnp.zeros((SL - N_HEADS * L, GROUP), F32))
        return jnp.concatenate(parts, axis=0)

    at_s = stack(-kk * g_prev)
    bt_s = stack(kka * g_inv)
    kt_s = stack(k2 * g_inv)
    rt_s = stack(r * jnp.exp(cum))
    bg_s = stack(kka * g_rel)
    kg_s = stack(k2 * g_rel)
    v_s = stack(v)
    return at_s, bt_s, kt_s, rt_s, bg_s, kg_s, v_s, cend, r, k2, v, g


def _mm2_exact_rhs_lhs(a_bf16, b):
    bh, bl = _split(b)
    d = functools.partial(jnp.dot, preferred_element_type=F32)
    return d(a_bf16, bh) + d(a_bf16, bl)


def _rwkv(rw, shift, st0, p, L, nseq=2):
    b, t, w = rw.shape
    const = lambda i, j: (0, 0)
    vec = pl.BlockSpec((1, GROUP), const)
    low = pl.BlockSpec((RWKV_LOW, GROUP), const)
    return pl.pallas_call(
        _rwkv_kernel,
        grid=(b // nseq, t // L),
        in_specs=[pl.BlockSpec((nseq, L, w), lambda i, j: (i, j, 0)),
                  pl.BlockSpec((nseq, 1, w), lambda i, j: (i, 0, 0)),
                  pl.BlockSpec((nseq, GROUP, GROUP), lambda i, j: (i, 0, 0)),
                  pl.BlockSpec((1, w), const),
                  vec, vec, vec, vec, vec, vec, vec, low, low, low],
        out_specs=[pl.BlockSpec((nseq, L, GROUP), lambda i, j: (i, j, 0)),
                   pl.BlockSpec((nseq, GROUP, GROUP), lambda i, j: (i, 0, 0))],
        out_shape=[jax.ShapeDtypeStruct((b, t, GROUP), F32),
                   jax.ShapeDtypeStruct((b, GROUP, GROUP), F32)],
        scratch_shapes=[pltpu.VMEM((nseq, GROUP, GROUP), F32),
                        pltpu.VMEM((nseq, 1, w), F32)],
        compiler_params=_cparams(2),
        name="rwkv",
    )(rw, shift, st0, p["mu"], p["w0"], p["a0"], p["kkw"], p["ka"], p["rk"], p["lng"], p["lnb"],
      p["w2"], p["a2"], p["g2"])


def _state_to_blockdiag(s):
    b = s.shape[0]
    st = jnp.swapaxes(s, -1, -2)
    eye = jnp.eye(N_HEADS, dtype=s.dtype)
    return jnp.einsum("bhkv,hg->bhkgv", st, eye).reshape(b, GROUP, GROUP)


def _blockdiag_to_state(st):
    b = st.shape[0]
    x = st.reshape(b, N_HEADS, HEAD_DIM, N_HEADS, HEAD_DIM)
    diag = jnp.stack([x[:, h, :, h, :] for h in range(N_HEADS)], axis=1)
    return jnp.swapaxes(diag, -1, -2)


def _outproj_kernel(oa_ref, ob_ref, oc_ref, od_ref, x_ref, wo_ref, g_ref, wq_ref, x1_ref, h_ref, q_ref):
    acc = x_ref[...]
    for i, o_ref in enumerate((oa_ref, ob_ref, oc_ref, od_ref)):
        acc = acc + jnp.dot(o_ref[...].astype(BF16), wo_ref[i], preferred_element_type=F32)
    x1_ref[...] = acc
    ms = jnp.mean(acc * acc, axis=-1, keepdims=True)
    h = (acc * lax.rsqrt(ms + RMS_EPS) * g_ref[...]).astype(BF16)
    h_ref[...] = h
    q_ref[...] = jnp.dot(h, wq_ref[...], preferred_element_type=F32)


def _outproj(oa, ob, oc, od, x, wo, g, wq, tm):
    n, d = x.shape
    dq = wq.shape[1]
    mix = pl.BlockSpec((tm, GROUP), lambda i: (i, 0))
    return pl.pallas_call(
        _outproj_kernel,
        grid=(n // tm,),
        in_specs=[mix, mix, mix, mix,
                  pl.BlockSpec((tm, d), lambda i: (i, 0)),
                  pl.BlockSpec((4, GROUP, d), lambda i: (0, 0, 0)),
                  pl.BlockSpec((1, d), lambda i: (0, 0)),
                  pl.BlockSpec((d, dq), lambda i: (0, 0))],
        out_specs=[pl.BlockSpec((tm, d), lambda i: (i, 0)),
                   pl.BlockSpec((tm, d), lambda i: (i, 0)),
                   pl.BlockSpec((tm, dq), lambda i: (i, 0))],
        out_shape=[jax.ShapeDtypeStruct((n, d), F32),
                   jax.ShapeDtypeStruct((n, d), BF16),
                   jax.ShapeDtypeStruct((n, dq), F32)],
        compiler_params=_cparams(1),
        name="outproj",
    )(oa, ob, oc, od, x, wo, g, wq)


_CAND_PIECES = ((0, 0, 8), (0, 8, 8), (1, 0, 8), (2, 0, 5), (3, 0, 4), (4, 0, 3), (5, 0, 2),
                (6, 0, 2), (7, 0, 2))
_CAND_ROWS = 8 * (len(_CAND_PIECES) + 1)


def _pop_max(x, ridx, n_rows):
    mx = jnp.max(x, axis=0, keepdims=True)
    first = jnp.min(jnp.where(x == mx, ridx, float(n_rows)), axis=0, keepdims=True)
    return jnp.where(ridx == first, -jnp.inf, x), mx


def _peer_select_kernel(q_ref, k1_ref, k2_ref, s1_ref, s2_ref, thr_ref, stat_ref,
                        xs_ref, v1_ref, v2_ref, cs_ref, cv_ref):
    tn = q_ref.shape[0]
    n_lt = tn // LANES
    q = q_ref[...].astype(BF16)
    nt = functools.partial(lax.dot_general, dimension_numbers=(((1,), (1,)), ((), ())),
                           preferred_element_type=F32)
    s1_all = nt(k1_ref[...], q)
    s2_all = nt(k2_ref[...], q)

    ridx = lax.broadcasted_iota(jnp.int32, (PEER_NKEYS, LANES), 0).astype(F32)
    cidx = ridx[0:_CAND_ROWS, :]
    sub = lax.broadcasted_iota(jnp.int32, (8, LANES), 0)
    for lt in range(n_lt):
        s1 = s1_all[:, lt * LANES:(lt + 1) * LANES]
        s2 = s2_all[:, lt * LANES:(lt + 1) * LANES]
        s1_ref[lt, 0] = s1
        s2_ref[lt, 0] = s2
        xs_ref[2 * lt] = s1
        xs_ref[2 * lt + 1] = s2

    def top_body(r, carry):
        for lt in range(n_lt):
            x1, m1 = _pop_max(xs_ref[2 * lt], ridx, PEER_NKEYS)
            x2, m2 = _pop_max(xs_ref[2 * lt + 1], ridx, PEER_NKEYS)
            xs_ref[2 * lt] = x1
            xs_ref[2 * lt + 1] = x2
            v1_ref[lt, pl.ds(r, 1), :] = m1
            v2_ref[lt, pl.ds(r, 1), :] = m2
        return carry

    lax.fori_loop(0, PEER_TOPK, top_body, 0)

    for lt in range(n_lt):
        v1 = v1_ref[lt]
        v2 = v2_ref[lt]
        for i, (a, b0, nb) in enumerate(_CAND_PIECES):
            cs_ref[lt, 8 * i:8 * i + 8, :] = jnp.where(sub < nb, v1[a:a + 1, :] + v2[b0:b0 + 8, :], -jnp.inf)
        i = len(_CAND_PIECES)
        cs_ref[lt, 8 * i:8 * i + 8, :] = v1[8:16, :] + v2[0:1, :]

    def cand_body(r, carry):
        for lt in range(n_lt):
            cand, mx = _pop_max(cs_ref[lt], cidx, _CAND_ROWS)
            cs_ref[lt] = cand
            cv_ref[lt, pl.ds(r, 1), :] = mx
        return carry

    lax.fori_loop(0, PEER_TOPK, cand_body, 0)

    for lt in range(n_lt):
        v1 = v1_ref[lt]
        v2 = v2_ref[lt]
        cv = cv_ref[lt]
        tau = cv[PEER_TOPK - 1:PEER_TOPK, :]
        z = jnp.sum(jnp.exp(cv - cv[0:1, :]), axis=0, keepdims=True)
        stat_ref[lt, 0] = jnp.concatenate(
            [tau, v1[0:1, :], v2[0:1, :], 1.0 / z, jnp.zeros((4, LANES), F32)], axis=0)
        s1 = s1_ref[lt, 0]
        thr = jnp.full((PEER_NKEYS, LANES), jnp.inf, F32)
        for b in range(PEER_TOPK):
            vb = v2[b:b + 1, :]
            thr = jnp.where((s1 + vb) >= tau, vb, thr)
        thr_ref[lt, 0] = thr


def _peer_select(q, k1p, k2p, tn):
    n, dq = q.shape
    qd = dq // PEER_HEADS
    n_lt = tn // LANES
    keyed = pl.BlockSpec((n_lt, 1, PEER_NKEYS, LANES), lambda i, h: (i, h, 0, 0))
    keyed_shape = jax.ShapeDtypeStruct((n // LANES, PEER_HEADS, PEER_NKEYS, LANES), F32)
    return pl.pallas_call(
        _peer_select_kernel,
        grid=(n // tn, PEER_HEADS),
        in_specs=[pl.BlockSpec((tn, qd), lambda i, h: (i, h)),
                  pl.BlockSpec((PEER_NKEYS, qd), lambda i, h: (0, 0)),
                  pl.BlockSpec((PEER_NKEYS, qd), lambda i, h: (0, 0))],
        out_specs=[keyed, keyed, keyed,
                   pl.BlockSpec((n_lt, 1, 8, LANES), lambda i, h: (i, h, 0, 0))],
        out_shape=[keyed_shape, keyed_shape, keyed_shape,
                   jax.ShapeDtypeStruct((n // LANES, PEER_HEADS, 8, LANES), F32)],
        scratch_shapes=[pltpu.VMEM((2 * n_lt, PEER_NKEYS, LANES), F32),
                        pltpu.VMEM((n_lt, PEER_TOPK, LANES), F32),
                        pltpu.VMEM((n_lt, PEER_TOPK, LANES), F32),
                        pltpu.VMEM((n_lt, _CAND_ROWS, LANES), F32),
                        pltpu.VMEM((n_lt, PEER_TOPK, LANES), F32)],
        compiler_params=_cparams(2),
        name="peer_select",
    )(q, k1p, k2p)


def _peer_dense_kernel(h_ref, u_ref, vt_ref, s1_ref, s2_ref, thr_ref, stat_ref, x1_ref, y_ref,
                       acc_ref, e1_ref, e2_ref, act_ref, w_ref):
    j = pl.program_id(1)
    nj = pl.num_programs(1)
    eb = u_ref.shape[0]
    n_lt = s2_ref.shape[0]
    ne1 = eb // PEER_NKEYS

    @pl.when(j == 0)
    def _():
        acc_ref[...] = jnp.zeros(acc_ref.shape, F32)
        for lt in range(n_lt):
            for h in range(PEER_HEADS):
                st = stat_ref[lt, h]
                e1_ref[lt, h] = jnp.exp(s1_ref[lt, h] - st[1:2, :]) * st[3:4, :]
                e2_ref[lt, h] = jnp.exp(s2_ref[lt, h] - st[2:3, :])

    act = lax.dot_general(u_ref[...], h_ref[...], (((1,), (1,)), ((), ())),
                          preferred_element_type=F32)
    for lt in range(n_lt):
        act_ref[lt] = act[:, lt * LANES:(lt + 1) * LANES]

    def first_key(e, carry):
        e1 = j * ne1 + e
        r0 = pl.multiple_of(e * PEER_NKEYS, PEER_NKEYS)

        def token_tile(lt, carry):
            gate = jnp.zeros((PEER_NKEYS, LANES), F32)
            for h in range(PEER_HEADS):
                chosen = s2_ref[lt, h] >= thr_ref[lt, h, pl.ds(e1, 1), :]
                gate = gate + jnp.where(chosen, e2_ref[lt, h], 0.0) * e1_ref[lt, h, pl.ds(e1, 1), :]
            a = act_ref[lt, pl.ds(r0, PEER_NKEYS), :]
            gelu = 0.5 * a * (1.0 + lax.erf(a * (2.0 ** -0.5)))
            w_ref[lt, pl.ds(r0, PEER_NKEYS), :] = (gate * gelu).astype(BF16)
            return carry

        return lax.fori_loop(0, n_lt, token_tile, carry)

    lax.fori_loop(0, ne1, first_key, 0)
    w = jnp.concatenate([w_ref[lt] for lt in range(n_lt)], axis=1)
    acc_ref[...] += jnp.dot(vt_ref[...], w, preferred_element_type=F32)

    @pl.when(j == nj - 1)
    def _():
        y_ref[...] = x1_ref[...] + acc_ref[...].T


def _peer_dense(h, u, vt, s1, s2, thr, stat, x1, tn, eb):
    n, d = x1.shape
    ne = u.shape[0]
    n_lt = tn // LANES
    keyed = pl.BlockSpec((n_lt, PEER_HEADS, PEER_NKEYS, LANES), lambda i, j: (i, 0, 0, 0))
    return pl.pallas_call(
        _peer_dense_kernel,
        grid=(n // tn, ne // eb),
        in_specs=[pl.BlockSpec((tn, d), lambda i, j: (i, 0)),
                  pl.BlockSpec((eb, d), lambda i, j: (j, 0)),
                  pl.BlockSpec((d, eb), lambda i, j: (0, j)),
                  keyed, keyed, keyed,
                  pl.BlockSpec((n_lt, PEER_HEADS, 8, LANES), lambda i, j: (i, 0, 0, 0)),
                  pl.BlockSpec((tn, d), lambda i, j: (i, 0))],
        out_specs=[pl.BlockSpec((tn, d), lambda i, j: (i, 0))],
        out_shape=[jax.ShapeDtypeStruct((n, d), F32)],
        scratch_shapes=[pltpu.VMEM((d, tn), F32),
                        pltpu.VMEM((n_lt, PEER_HEADS, PEER_NKEYS, LANES), F32),
                        pltpu.VMEM((n_lt, PEER_HEADS, PEER_NKEYS, LANES), F32),
                        pltpu.VMEM((n_lt, eb, LANES), F32),
                        pltpu.VMEM((n_lt, eb, LANES), BF16)],
        compiler_params=_cparams(2),
        name="peer_dense",
    )(h, u, vt, s1, s2, thr, stat, x1)[0]


def _rope_tables(pos):
    half = HEAD_DIM // 2
    inv_freq = ROPE_THETA ** (-jnp.arange(half, dtype=F32) / half)
    ang = pos.astype(F32)[:, None] * inv_freq[None, :]
    cos = jnp.tile(jnp.cos(ang), (1, 2 * N_HEADS))
    sin = jnp.sin(ang)
    sin_signed = jnp.tile(jnp.concatenate([-sin, sin], axis=1), (1, N_HEADS))
    return cos, sin_signed


def _row_tile(n, pref):
    return pref if n % pref == 0 else n


def _seq_group(b, pref):
    return math.gcd(b, pref)


def _layer_params(l, norm_mix, w_in, q_norm, k_norm, conf_dw, conf_dw_b, conf_ln_g, conf_ln_b,
                  rwkv_mu, rwkv_w0, rwkv_w2, rwkv_a0, rwkv_a2, rwkv_g2, rwkv_kk, rwkv_ka, rwkv_rk,
                  rwkv_ln_g, rwkv_ln_b, sc_w, w_out, norm_ffn, peer_wq, peer_k1, peer_k2, peer_u, peer_v):
    row = lambda a: a[l].reshape(1, -1)
    att_c = 3 * GROUP
    conf_c = 2 * GROUP
    rw_c = rwkv_mu.shape[1]
    wi = w_in[l].astype(BF16)
    bounds = [0, att_c, att_c + conf_c, att_c + conf_c + rw_c, wi.shape[1]]
    rd, ra, rg = rwkv_w2.shape[1], rwkv_a2.shape[1], rwkv_g2.shape[1]
    zeros = lambda r: jnp.zeros((r, GROUP), F32)
    half = peer_k1.shape[2]
    return dict(
        norm_mix=row(norm_mix),
        w_in=[wi[:, bounds[i]:bounds[i + 1]] for i in range(4)],
        qg=jnp.tile(row(q_norm), (1, N_HEADS)), kg=jnp.tile(row(k_norm), (1, N_HEADS)),
        dw=conf_dw[l], dwb=row(conf_dw_b), clng=row(conf_ln_g), clnb=row(conf_ln_b), scw=sc_w[l],
        rwkv=dict(mu=row(rwkv_mu), w0=row(rwkv_w0), a0=row(rwkv_a0), kkw=row(rwkv_kk), ka=row(rwkv_ka),
                  rk=row(rwkv_rk), lng=row(rwkv_ln_g), lnb=row(rwkv_ln_b),
                  w2=jnp.concatenate([rwkv_w2[l], zeros(ra + rg)], axis=0),
                  a2=jnp.concatenate([zeros(rd), rwkv_a2[l], zeros(rg)], axis=0),
                  g2=jnp.concatenate([zeros(rd + ra), rwkv_g2[l]], axis=0)),
        w_out=w_out[l].astype(BF16).reshape(4, GROUP, -1),
        norm_ffn=row(norm_ffn), wq=peer_wq[l].astype(BF16),
        k1p=jnp.concatenate([peer_k1[l], jnp.zeros_like(peer_k1[l])], axis=1).astype(BF16),
        k2p=jnp.concatenate([jnp.zeros_like(peer_k2[l]), peer_k2[l]], axis=1).astype(BF16),
        u=peer_u[l].astype(BF16), vt=peer_v[l].astype(BF16).T,
    )


def _mix_and_ffn(x, oa, ob, oc, od, p):
    b, t, d = x.shape
    n = b * t
    flat = lambda a: a.reshape(n, a.shape[-1])
    tm = _row_tile(n, 512)
    x1, h2, q = _outproj(flat(oa), flat(ob), flat(oc), flat(od), flat(x), p["w_out"], p["norm_ffn"],
                         p["wq"], tm)
    tn = _row_tile(n, 512)
    s1, s2, thr, stat = _peer_select(q, p["k1p"], p["k2p"], tn)
    y = _peer_dense(h2, p["u"], p["vt"], s1, s2, thr, stat, x1, tn, 512)
    return y.reshape(b, t, d)


def _common_mixers(x, p, conf_buf, shift_prev, wkv_prev, sc_buf, chunk, nseq, conv_tile):
    b, t, d = x.shape
    n = b * t
    att, conf, rw, sc = _proj(x.reshape(n, d), p["norm_mix"], p["w_in"], _row_tile(n, 512))
    att = att.reshape(b, t, -1)
    conf = conf.reshape(b, t, -1)
    rw = rw.reshape(b, t, -1)
    sc = sc.reshape(b, t, -1)
    ob, od, conf_new, sc_new = _conv(conf, sc, conf_buf, sc_buf, p["dw"], p["dwb"], p["clng"], p["clnb"],
                                     p["scw"], conv_tile)
    oc, st_out = _rwkv(rw, shift_prev[:, None, :], _state_to_blockdiag(wkv_prev), p["rwkv"], chunk, nseq)
    return att, ob, oc, od, conf_new, rw[:, -1, :], _blockdiag_to_state(st_out), sc_new


def kernel(x_prompt, x_sample, cache_k, cache_v, page_table, state_conf_conv, state_rwkv_shift, state_rwkv_wkv, state_short_conv, norm_mix, w_in, q_norm, k_norm, conf_dw, conf_dw_b, conf_ln_g, conf_ln_b, rwkv_mu, rwkv_w0, rwkv_w2, rwkv_a0, rwkv_a2, rwkv_g2, rwkv_kk, rwkv_ka, rwkv_rk, rwkv_ln_g, rwkv_ln_b, sc_w, w_out, norm_ffn, peer_wq, peer_k1, peer_k2, peer_u, peer_v):
    depth = w_in.shape[0]
    bp, seq, _ = x_prompt.shape
    db, dec_seq, _ = x_sample.shape
    n_pool = cache_k.shape[1]
    past_len = page_table.shape[1] * PAGE
    ck = cache_k.reshape(depth * n_pool, PAGE, GROUP)
    cv = cache_v.reshape(depth * n_pool, PAGE, GROUP)
    cos_p, sin_p = _rope_tables(jnp.arange(seq))
    cos_s, sin_s = _rope_tables(past_len + jnp.arange(dec_seq))
    dt = x_prompt.dtype

    yp, ys = x_prompt, x_sample
    outs_p = [[] for _ in range(6)]
    outs_s = [[] for _ in range(6)]
    for l in range(depth):
        p = _layer_params(l, norm_mix, w_in, q_norm, k_norm, conf_dw, conf_dw_b, conf_ln_g, conf_ln_b,
                          rwkv_mu, rwkv_w0, rwkv_w2, rwkv_a0, rwkv_a2, rwkv_g2, rwkv_kk, rwkv_ka,
                          rwkv_rk, rwkv_ln_g, rwkv_ln_b, sc_w, w_out, norm_ffn, peer_wq, peer_k1,
                          peer_k2, peer_u, peer_v)
        att, ob, oc, od, conf_new, shift_new, wkv_new, sc_new = _common_mixers(
            yp, p,
            jnp.zeros((bp, CONF_TAPS - 1, GROUP), dt),
            jnp.zeros((bp, rwkv_mu.shape[1]), dt),
            jnp.zeros((bp, N_HEADS, HEAD_DIM, HEAD_DIM), dt),
            jnp.zeros((bp, SC_TAPS - 1, GROUP), dt),
            chunk=64, nseq=_seq_group(bp, 4), conv_tile=_row_tile(seq, 512))
        oa, k_rot = _moba_prompt(att, cos_p, sin_p, p["qg"], p["kg"])
        new_p = (k_rot.reshape(bp, seq, N_HEADS, HEAD_DIM),
                 att[:, :, 2 * GROUP:].reshape(bp, seq, N_HEADS, HEAD_DIM),
                 conf_new, shift_new, wkv_new, sc_new)
        yp = _mix_and_ffn(yp, oa, ob, oc, od, p)

        att, ob, oc, od, conf_new, shift_new, wkv_new, sc_new = _common_mixers(
            ys, p, state_conf_conv[l], state_rwkv_shift[l], state_rwkv_wkv[l], state_short_conv[l],
            chunk=dec_seq, nseq=_seq_group(db, 8), conv_tile=dec_seq)
        sel, q_rot, k_rot = _decode_sel(att, cos_s, sin_s, p["qg"], p["kg"], ck, page_table, l * n_pool)
        v_new = att[:, :, 2 * GROUP:]
        oa = _decode_att(sel, q_rot, k_rot, v_new, ck, cv, page_table, l * n_pool)
        new_s = (k_rot.reshape(db, dec_seq, N_HEADS, HEAD_DIM),
                 v_new.reshape(db, dec_seq, N_HEADS, HEAD_DIM),
                 conf_new, shift_new, wkv_new, sc_new)
        ys = _mix_and_ffn(ys, oa, ob, oc, od, p)

        for lst, a in zip(outs_p, new_p):
            lst.append(a)
        for lst, a in zip(outs_s, new_s):
            lst.append(a)
    pk, pv, pconf, pshift, pwkv, psc = [jnp.stack(a) for a in outs_p]
    sk, sv, sconf, sshift, swkv, ssc = [jnp.stack(a) for a in outs_s]
    return (yp, ys, pk, pv, sk, sv, pconf, sconf, pshift, sshift, pwkv, swkv, psc, ssc)
```

```python
import functools
import math

import jax
import jax.numpy as jnp
from jax import lax
from jax.experimental import pallas as pl
from jax.experimental.pallas import tpu as pltpu

F32 = jnp.float32
BF16 = jnp.bfloat16

HEAD_DIM = 64
GROUP = 256
N_HEADS = GROUP // HEAD_DIM
MOBA_BLOCK = 256
MOBA_TOPK = 3
ROPE_THETA = 10000.0
PAGE = 128
CONF_TAPS = 31
SC_TAPS = 3
RWKV_LOW = 128
PEER_HEADS = 8
PEER_NKEYS = 128
PEER_TOPK = 16
RMS_EPS = 1e-6
LN_EPS = 1e-5
RWKV_GN_EPS = 64e-5
NEG_INF = -1e30
LANES = 128
VMEM_LIMIT = 56 * 1024 * 1024

_ARB = "arbitrary"


def _cparams(n_axes, vmem=VMEM_LIMIT):
    return pltpu.CompilerParams(dimension_semantics=(_ARB,) * n_axes, vmem_limit_bytes=vmem)


def _mm(a, b):
    return jnp.dot(a.astype(BF16), b.astype(BF16), preferred_element_type=F32)


def _mm_nt(a, b):
    return lax.dot_general(a.astype(BF16), b.astype(BF16), (((1,), (1,)), ((), ())),
                           preferred_element_type=F32)


def _split(a):
    hi = a.astype(BF16)
    lo = (a - hi.astype(F32)).astype(BF16)
    return hi, lo


def _mm3(a, b):
    ah, al = _split(a)
    bh, bl = _split(b)
    d = functools.partial(jnp.dot, preferred_element_type=F32)
    return d(ah, bh) + d(ah, bl) + d(al, bh)


def _mm3_nt(a, b):
    ah, al = _split(a)
    bh, bl = _split(b)
    d = functools.partial(lax.dot_general, dimension_numbers=(((1,), (1,)), ((), ())),
                          preferred_element_type=F32)
    return d(ah, bh) + d(ah, bl) + d(al, bh)


def _mm2_exact_rhs(a, b_bf16):
    ah, al = _split(a)
    d = functools.partial(jnp.dot, preferred_element_type=F32)
    return d(ah, b_bf16) + d(al, b_bf16)


def _sigmoid(x):
    return 1.0 / (1.0 + jnp.exp(-x))


def _head_of_lane(shape, axis):
    return lax.broadcasted_iota(jnp.int32, shape, axis) // HEAD_DIM


def _head_sum_matrix():
    r = _head_of_lane((GROUP, GROUP), 0)
    c = _head_of_lane((GROUP, GROUP), 1)
    return jnp.where(r == c, 1.0, 0.0).astype(BF16)


def _proj_kernel(x_ref, g_ref, wa_ref, wc_ref, wr_ref, ws_ref, oa_ref, oc_ref, or_ref, os_ref):
    x = x_ref[...]
    ms = jnp.mean(x * x, axis=-1, keepdims=True)
    h = (x * lax.rsqrt(ms + RMS_EPS) * g_ref[...]).astype(BF16)
    for w_ref, o_ref in ((wa_ref, oa_ref), (wc_ref, oc_ref), (wr_ref, or_ref), (ws_ref, os_ref)):
        o_ref[...] = jnp.dot(h, w_ref[...], preferred_element_type=F32)


def _proj(x, g, ws, tm):
    n, d = x.shape
    widths = [w.shape[1] for w in ws]
    return pl.pallas_call(
        _proj_kernel,
        grid=(n // tm,),
        in_specs=[pl.BlockSpec((tm, d), lambda i: (i, 0)),
                  pl.BlockSpec((1, d), lambda i: (0, 0))]
                 + [pl.BlockSpec((d, wd), lambda i: (0, 0)) for wd in widths],
        out_specs=[pl.BlockSpec((tm, wd), lambda i: (i, 0)) for wd in widths],
        out_shape=[jax.ShapeDtypeStruct((n, wd), F32) for wd in widths],
        compiler_params=_cparams(1),
        name="proj",
    )(x, g, *ws)


def _head_rms(x, gain, bd):
    ms = _mm2_exact_rhs(x * x, bd) * (1.0 / HEAD_DIM)
    return x * lax.rsqrt(ms + RMS_EPS) * gain


def _rope(x, cos, sin_signed):
    lane = lax.broadcasted_iota(jnp.int32, x.shape, 1) % HEAD_DIM
    lo = pltpu.roll(x, GROUP - HEAD_DIM // 2, axis=1)
    hi = pltpu.roll(x, HEAD_DIM // 2, axis=1)
    rot = jnp.where(lane < HEAD_DIM // 2, lo, hi)
    return x * cos + rot * sin_signed


def _top_rows(g, n_rows, k):
    ridx = lax.broadcasted_iota(jnp.int32, g.shape, 0).astype(F32)
    sel = jnp.zeros(g.shape, F32)
    cur = g
    for _ in range(k):
        mx = jnp.max(cur, axis=0, keepdims=True)
        first = jnp.min(jnp.where(cur == mx, ridx, float(n_rows)), axis=0, keepdims=True)
        hit = ridx == first
        sel = jnp.where(hit, 1.0, sel)
        cur = jnp.where(hit, -jnp.inf, cur)
    return sel


def _moba_prompt_kernel(att_ref, cos_ref, sin_ref, qg_ref, kg_ref, oa_ref, kr_ref,
                        kb_ref, vt_ref, km_ref, sel_ref, qm_ref):
    qi = pl.program_id(1)
    nblk = kb_ref.shape[0]
    blk = MOBA_BLOCK
    bd = _head_sum_matrix()

    @pl.when(qi == 0)
    def _():
        km_ref[...] = jnp.zeros(km_ref.shape, F32)

    a = att_ref[0]
    cos = cos_ref[...]
    sin = sin_ref[...]
    qr = _rope(_head_rms(a[:, 0:GROUP], qg_ref[...], bd), cos, sin)
    kr = _rope(_head_rms(a[:, GROUP:2 * GROUP], kg_ref[...], bd), cos, sin)
    v = a[:, 2 * GROUP:3 * GROUP]
    kr_ref[0] = kr
    kb_ref[qi] = kr.astype(BF16)
    vt_ref[qi] = v.T.astype(BF16)
    km_ref[pl.ds(qi, 1), :] = jnp.mean(kr, axis=0, keepdims=True)

    qs = qr * (HEAD_DIM ** -0.5)
    lane_head = _head_of_lane((blk, GROUP), 1)
    km = km_ref[...]
    km_head = _head_of_lane(km.shape, 1)
    blk_row = lax.broadcasted_iota(jnp.int32, (nblk, blk), 0)
    krow = lax.broadcasted_iota(jnp.int32, (blk, blk), 0)
    qcol = lax.broadcasted_iota(jnp.int32, (blk, blk), 1)
    kcur = kb_ref[qi]
    heads = range(N_HEADS)
    head_rows = [slice(h * HEAD_DIM, (h + 1) * HEAD_DIM) for h in heads]
    nt = functools.partial(lax.dot_general, dimension_numbers=(((1,), (1,)), ((), ())),
                           preferred_element_type=F32)
    qms = [jnp.where(lane_head == h, qs, 0.0) for h in heads]
    for h in heads:
        qm_ref[h] = qms[h].astype(BF16)
    past = blk_row < qi
    gates = [_mm3_nt(jnp.where(km_head == h, km, 0.0), qms[h]) for h in heads]
    for h in heads:
        sel = _top_rows(jnp.where(past, gates[h], NEG_INF), nblk, MOBA_TOPK)
        sel_ref[h] = jnp.where(past, sel, 0.0)
    sts = [jnp.where(krow <= qcol, nt(kcur, qm_ref[h]), NEG_INF) for h in heads]
    ms = [jnp.max(st, axis=0, keepdims=True) for st in sts]
    ps = [jnp.exp(st - m) for st, m in zip(sts, ms)]
    ls = [jnp.sum(p, axis=0, keepdims=True) for p in ps]
    accs = [jnp.dot(vt_ref[qi, head_rows[h], :], ps[h].astype(BF16), preferred_element_type=F32)
            for h in heads]

    def past_block(j, carry):
        ms, ls, accs = carry
        kj = kb_ref[j]
        sts = [jnp.where(sel_ref[h, pl.ds(j, 1), :] > 0.0, nt(kj, qm_ref[h]), NEG_INF) for h in heads]
        m_new = [jnp.maximum(m, jnp.max(st, axis=0, keepdims=True)) for m, st in zip(ms, sts)]
        alpha = [jnp.exp(m - mn) for m, mn in zip(ms, m_new)]
        ps = [jnp.exp(st - mn) for st, mn in zip(sts, m_new)]
        ls = [a * l + jnp.sum(p, axis=0, keepdims=True) for a, l, p in zip(alpha, ls, ps)]
        accs = [alpha[h] * accs[h] + jnp.dot(vt_ref[j, head_rows[h], :], ps[h].astype(BF16),
                                             preferred_element_type=F32) for h in heads]
        return m_new, ls, accs

    ms, ls, accs = lax.fori_loop(0, qi, past_block, (ms, ls, accs))
    oa_ref[0] = jnp.concatenate([acc / l for acc, l in zip(accs, ls)], axis=0).T


def _moba_prompt(att, cos, sin, qg, kg):
    b, t, _ = att.shape
    nblk = t // MOBA_BLOCK
    blk = MOBA_BLOCK
    return pl.pallas_call(
        _moba_prompt_kernel,
        grid=(b, nblk),
        in_specs=[pl.BlockSpec((1, blk, 3 * GROUP), lambda i, j: (i, j, 0)),
                  pl.BlockSpec((blk, GROUP), lambda i, j: (j, 0)),
                  pl.BlockSpec((blk, GROUP), lambda i, j: (j, 0)),
                  pl.BlockSpec((1, GROUP), lambda i, j: (0, 0)),
                  pl.BlockSpec((1, GROUP), lambda i, j: (0, 0))],
        out_specs=[pl.BlockSpec((1, blk, GROUP), lambda i, j: (i, j, 0)),
                   pl.BlockSpec((1, blk, GROUP), lambda i, j: (i, j, 0))],
        out_shape=[jax.ShapeDtypeStruct((b, t, GROUP), F32),
                   jax.ShapeDtypeStruct((b, t, GROUP), F32)],
        scratch_shapes=[pltpu.VMEM((nblk, blk, GROUP), BF16),
                        pltpu.VMEM((nblk, GROUP, blk), BF16),
                        pltpu.VMEM((nblk, GROUP), F32),
                        pltpu.VMEM((N_HEADS, nblk, blk), F32),
                        pltpu.VMEM((N_HEADS, blk, GROUP), BF16)],
        compiler_params=_cparams(2),
        name="moba_prompt",
    )(att, cos, sin, qg, kg)


def _stack_heads(x):
    lane_head = _head_of_lane(x.shape, 1)
    return jnp.concatenate([jnp.where(lane_head == h, x, 0.0) for h in range(N_HEADS)], axis=0)


def _decode_sel_kernel(pt_ref, att_ref, cos_ref, sin_ref, qg_ref, kg_ref, *rest, pages_per_step):
    page_refs = rest[:pages_per_step]
    sel_ref, qr_ref, kr_ref, q2_ref, gate_ref = rest[pages_per_step:]
    s = pl.program_id(1)
    n_steps = pl.num_programs(1)
    blocks_per_step = pages_per_step * PAGE // MOBA_BLOCK
    n_past = n_steps * blocks_per_step
    pages_per_block = MOBA_BLOCK // PAGE
    rows = gate_ref.shape[0]
    bd = _head_sum_matrix()

    @pl.when(s == 0)
    def _():
        gate_ref[...] = jnp.zeros(gate_ref.shape, F32)
        a = att_ref[0]
        qr = _rope(_head_rms(a[:, 0:GROUP], qg_ref[...], bd), cos_ref[...], sin_ref[...])
        qr_ref[0] = qr
        kr_ref[0] = _rope(_head_rms(a[:, GROUP:2 * GROUP], kg_ref[...], bd), cos_ref[...], sin_ref[...])
        hi, lo = _split(_stack_heads(qr * (HEAD_DIM ** -0.5)))
        q2_ref[...] = jnp.concatenate([hi, lo], axis=0)

    lane = lax.broadcasted_iota(jnp.int32, gate_ref.shape, 1)
    for bi in range(blocks_per_step):
        tot = jnp.zeros((rows, 1), F32)
        for pi in range(pages_per_block):
            kt = page_refs[bi * pages_per_block + pi][0, 0].reshape(GROUP, PAGE).astype(BF16)
            sc = jnp.dot(q2_ref[...], kt, preferred_element_type=F32)
            tot = tot + jnp.sum(sc[0:rows, :] + sc[rows:2 * rows, :], axis=1, keepdims=True)
        gate_ref[...] += jnp.where(lane == s * blocks_per_step + bi, tot * (1.0 / MOBA_BLOCK), 0.0)

    @pl.when(s == n_steps - 1)
    def _():
        gate = gate_ref[...]
        past = lane < n_past
        cur = jnp.where(past, gate, NEG_INF)
        lidx = lane.astype(F32)
        sel = jnp.zeros(gate.shape, F32)
        for _ in range(MOBA_TOPK):
            mx = jnp.max(cur, axis=1, keepdims=True)
            first = jnp.min(jnp.where(cur == mx, lidx, float(LANES)), axis=1, keepdims=True)
            hit = lidx == first
            sel = jnp.where(hit, 1.0, sel)
            cur = jnp.where(hit, -jnp.inf, cur)
        sel_ref[0] = jnp.where(past, sel, 0.0)


def _decode_sel(att, cos, sin, qg, kg, cache_kt, page_table, layer, pages_per_step=8):
    db, t, _ = att.shape
    n_pages = page_table.shape[1]
    n_steps = n_pages // pages_per_step
    rows = N_HEADS * t
    assert n_pages * PAGE // MOBA_BLOCK <= LANES

    def page_map(i):
        return lambda b, s, pt: (layer, pt[b, s * pages_per_step + i], 0, 0, 0)

    grid_spec = pltpu.PrefetchScalarGridSpec(
        num_scalar_prefetch=1,
        grid=(db, n_steps),
        in_specs=[pl.BlockSpec((1, t, 3 * GROUP), lambda b, s, pt: (b, 0, 0)),
                  pl.BlockSpec((t, GROUP), lambda b, s, pt: (0, 0)),
                  pl.BlockSpec((t, GROUP), lambda b, s, pt: (0, 0)),
                  pl.BlockSpec((1, GROUP), lambda b, s, pt: (0, 0)),
                  pl.BlockSpec((1, GROUP), lambda b, s, pt: (0, 0))]
                 + [pl.BlockSpec((1, 1, N_HEADS, HEAD_DIM, PAGE), page_map(i)) for i in range(pages_per_step)],
        out_specs=[pl.BlockSpec((1, rows, LANES), lambda b, s, pt: (b, 0, 0)),
                   pl.BlockSpec((1, t, GROUP), lambda b, s, pt: (b, 0, 0)),
                   pl.BlockSpec((1, t, GROUP), lambda b, s, pt: (b, 0, 0))],
        scratch_shapes=[pltpu.VMEM((2 * rows, GROUP), BF16),
                        pltpu.VMEM((rows, LANES), F32)],
    )
    return pl.pallas_call(
        functools.partial(_decode_sel_kernel, pages_per_step=pages_per_step),
        grid_spec=grid_spec,
        out_shape=[jax.ShapeDtypeStruct((db, N_HEADS * t, LANES), F32),
                   jax.ShapeDtypeStruct((db, t, GROUP), F32),
                   jax.ShapeDtypeStruct((db, t, GROUP), F32)],
        compiler_params=_cparams(2),
        name="decode_sel",
    )(page_table, att, cos, sin, qg, kg, *([cache_kt] * pages_per_step))


def _decode_att_kernel(pt_ref, sel_ref, qr_ref, kn_ref, vn_ref, *rest, pages_per_step):
    k_refs = rest[:pages_per_step]
    v_refs = rest[pages_per_step:2 * pages_per_step]
    oa_ref, qbd_ref, acc_ref, m_ref, l_ref = rest[2 * pages_per_step:]
    j = pl.program_id(1)
    n_steps = pl.num_programs(1)
    t = qr_ref.shape[1]
    blocks_per_step = pages_per_step * PAGE // MOBA_BLOCK

    @pl.when(j == 0)
    def _():
        qbd_ref[...] = _stack_heads(qr_ref[0] * (HEAD_DIM ** -0.5)).astype(BF16)
        pad = jnp.zeros((MOBA_BLOCK - t, GROUP), F32)
        kn = jnp.concatenate([kn_ref[0], pad], axis=0).astype(BF16)
        vn = jnp.concatenate([vn_ref[0], pad], axis=0).astype(BF16)
        st = lax.dot_general(qbd_ref[...], kn, (((1,), (1,)), ((), ())), preferred_element_type=F32)
        qpos = lax.broadcasted_iota(jnp.int32, st.shape, 0) % t
        kpos = lax.broadcasted_iota(jnp.int32, st.shape, 1)
        st = jnp.where(kpos <= qpos, st, NEG_INF)
        m = jnp.max(st, axis=1, keepdims=True)
        p = jnp.exp(st - m)
        m_ref[...] = m
        l_ref[...] = jnp.sum(p, axis=1, keepdims=True)
        acc_ref[...] = jnp.dot(p.astype(BF16), vn, preferred_element_type=F32)

    qbd = qbd_ref[...]
    pages_per_block = MOBA_BLOCK // PAGE
    sel = sel_ref[0]
    lane = lax.broadcasted_iota(jnp.int32, sel.shape, 1)
    masked = []
    for pi, r in enumerate(k_refs):
        sc = jnp.dot(qbd, r[0, 0].reshape(GROUP, PAGE).astype(BF16), preferred_element_type=F32)
        blk = j * blocks_per_step + pi // pages_per_block
        chosen = jnp.sum(jnp.where(lane == blk, sel, 0.0), axis=1, keepdims=True) > 0.0
        masked.append(jnp.where(chosen, sc, NEG_INF))
    st = jnp.concatenate(masked, axis=1)
    m_old = m_ref[...]
    m_new = jnp.maximum(m_old, jnp.max(st, axis=1, keepdims=True))
    alpha = jnp.exp(m_old - m_new)
    p32 = jnp.exp(st - m_new)
    p = p32.astype(BF16)
    m_ref[...] = m_new
    l_ref[...] = alpha * l_ref[...] + jnp.sum(p32, axis=1, keepdims=True)
    acc = alpha * acc_ref[...]
    for pi, r in enumerate(v_refs):
        acc = acc + lax.dot_general(p[:, pi * PAGE:(pi + 1) * PAGE], r[0, 0].reshape(GROUP, PAGE).astype(BF16),
                                    (((1,), (1,)), ((), ())), preferred_element_type=F32)
    acc_ref[...] = acc

    @pl.when(j == n_steps - 1)
    def _():
        o = acc_ref[...] / l_ref[...]
        lane_head = _head_of_lane((t, GROUP), 1)
        out = jnp.zeros((t, GROUP), F32)
        for h in range(N_HEADS):
            out = out + jnp.where(lane_head == h, o[h * t:(h + 1) * t, :], 0.0)
        oa_ref[0] = out


def _decode_att(sel, qr, kn, vn, cache_kt, cache_vt, page_table, layer, pages_per_step=8):
    db, t, _ = qr.shape
    n_steps = page_table.shape[1] // pages_per_step
    rows = N_HEADS * t

    def page_map(i):
        return lambda b, j, pt: (layer, pt[b, pages_per_step * j + i], 0, 0, 0)

    pages = [pl.BlockSpec((1, 1, N_HEADS, HEAD_DIM, PAGE), page_map(i)) for i in range(pages_per_step)]
    grid_spec = pltpu.PrefetchScalarGridSpec(
        num_scalar_prefetch=1,
        grid=(db, n_steps),
        in_specs=[pl.BlockSpec((1, rows, LANES), lambda b, j, pt: (b, 0, 0)),
                  pl.BlockSpec((1, t, GROUP), lambda b, j, pt: (b, 0, 0)),
                  pl.BlockSpec((1, t, GROUP), lambda b, j, pt: (b, 0, 0)),
                  pl.BlockSpec((1, t, GROUP), lambda b, j, pt: (b, 0, 0))] + pages + pages,
        out_specs=[pl.BlockSpec((1, t, GROUP), lambda b, j, pt: (b, 0, 0))],
        scratch_shapes=[pltpu.VMEM((rows, GROUP), BF16),
                        pltpu.VMEM((rows, GROUP), F32),
                        pltpu.VMEM((rows, 1), F32),
                        pltpu.VMEM((rows, 1), F32)],
    )
    return pl.pallas_call(
        functools.partial(_decode_att_kernel, pages_per_step=pages_per_step),
        grid_spec=grid_spec,
        out_shape=[jax.ShapeDtypeStruct((db, t, GROUP), F32)],
        compiler_params=_cparams(2),
        name="decode_att",
    )(page_table, sel, qr, kn, vn, *([cache_kt] * pages_per_step), *([cache_vt] * pages_per_step))[0]


def _conv_kernel(conf_ref, sc_ref, cbuf_ref, sbuf_ref, dw_ref, dwb_ref, lng_ref, lnb_ref, scw_ref,
                 ob_ref, od_ref, cnew_ref, snew_ref, extc_ref, exts_ref):
    ti = pl.program_id(1)
    nt = pl.num_programs(1)
    tt = conf_ref.shape[1]
    hc = 32
    hs = 8
    nbc = CONF_TAPS - 1
    nbs = SC_TAPS - 1

    @pl.when(ti == 0)
    def _():
        extc_ref[0:hc - nbc, :] = jnp.zeros((hc - nbc, GROUP), F32)
        extc_ref[hc - nbc:hc, :] = cbuf_ref[0]
        exts_ref[0:hs - nbs, :] = jnp.zeros((hs - nbs, GROUP), F32)
        exts_ref[hs - nbs:hs, :] = sbuf_ref[0]

    c = conf_ref[0]
    glu = c[:, 0:GROUP] * _sigmoid(c[:, GROUP:2 * GROUP])
    extc_ref[hc:hc + tt, :] = glu
    acc = jnp.zeros((tt, GROUP), F32) + dwb_ref[...]
    for j in range(CONF_TAPS):
        acc = acc + dw_ref[j:j + 1, :] * extc_ref[pl.ds(hc - nbc + j, tt), :]
    mu = jnp.mean(acc, axis=-1, keepdims=True)
    var = jnp.mean(jnp.square(acc - mu), axis=-1, keepdims=True)
    ln = (acc - mu) * lax.rsqrt(var + LN_EPS) * lng_ref[...] + lnb_ref[...]
    ob_ref[0] = ln * _sigmoid(ln)

    s = sc_ref[0]
    gated = s[:, GROUP:2 * GROUP] * s[:, 2 * GROUP:3 * GROUP]
    exts_ref[hs:hs + tt, :] = gated
    d = jnp.zeros((tt, GROUP), F32)
    for j in range(SC_TAPS):
        d = d + scw_ref[j:j + 1, :] * exts_ref[pl.ds(hs - nbs + j, tt), :]
    od_ref[0] = s[:, 0:GROUP] * d

    @pl.when(ti == nt - 1)
    def _():
        cnew_ref[0] = extc_ref[pl.ds(tt + hc - nbc, nbc), :]
        snew_ref[0] = exts_ref[pl.ds(tt + hs - nbs, nbs), :]

    tail_c = extc_ref[pl.ds(tt, hc), :]
    tail_s = exts_ref[pl.ds(tt, hs), :]
    extc_ref[0:hc, :] = tail_c
    exts_ref[0:hs, :] = tail_s


def _conv(conf, sc, cbuf, sbuf, dw, dwb, lng, lnb, scw, tt):
    b, t, _ = conf.shape
    nbc = CONF_TAPS - 1
    nbs = SC_TAPS - 1
    const = lambda i, j: (0, 0)
    return pl.pallas_call(
        _conv_kernel,
        grid=(b, t // tt),
        in_specs=[pl.BlockSpec((1, tt, 2 * GROUP), lambda i, j: (i, j, 0)),
                  pl.BlockSpec((1, tt, 3 * GROUP), lambda i, j: (i, j, 0)),
                  pl.BlockSpec((1, nbc, GROUP), lambda i, j: (i, 0, 0)),
                  pl.BlockSpec((1, nbs, GROUP), lambda i, j: (i, 0, 0)),
                  pl.BlockSpec((CONF_TAPS, GROUP), const),
                  pl.BlockSpec((1, GROUP), const),
                  pl.BlockSpec((1, GROUP), const),
                  pl.BlockSpec((1, GROUP), const),
                  pl.BlockSpec((SC_TAPS, GROUP), const)],
        out_specs=[pl.BlockSpec((1, tt, GROUP), lambda i, j: (i, j, 0)),
                   pl.BlockSpec((1, tt, GROUP), lambda i, j: (i, j, 0)),
                   pl.BlockSpec((1, nbc, GROUP), lambda i, j: (i, 0, 0)),
                   pl.BlockSpec((1, nbs, GROUP), lambda i, j: (i, 0, 0))],
        out_shape=[jax.ShapeDtypeStruct((b, t, GROUP), F32),
                   jax.ShapeDtypeStruct((b, t, GROUP), F32),
                   jax.ShapeDtypeStruct((b, nbc, GROUP), F32),
                   jax.ShapeDtypeStruct((b, nbs, GROUP), F32)],
        scratch_shapes=[pltpu.VMEM((32 + tt, GROUP), F32),
                        pltpu.VMEM((8 + tt, GROUP), F32)],
        compiler_params=_cparams(2),
        name="conv",
    )(conf, sc, cbuf, sbuf, dw, dwb, lng, lnb, scw)


def _rwkv_kernel(rw_ref, shift_ref, st0_ref, mu_ref, w0_ref, a0_ref, kkw_ref, ka_ref, rk_ref,
                 lng_ref, lnb_ref, w2_ref, a2_ref, g2_ref, oc_ref, sto_ref, st_ref, prev_ref):
    c = pl.program_id(1)

    @pl.when(c == 0)
    def _():
        st_ref[...] = st0_ref[...]
        prev_ref[...] = shift_ref[...]

    nseq, L, _ = rw_ref.shape
    SL = max(N_HEADS * L, LANES)
    seqs = range(nseq)
    each = lambda f, *cols: [f(*args) for args in zip(*cols)]
    P = [_rwkv_prep(s, rw_ref, mu_ref, w0_ref, a0_ref, kkw_ref, ka_ref, w2_ref, a2_ref, g2_ref, prev_ref)
         for s in seqs]
    at_s, bt_s, kt_s, rt_s, bg_s, kg_s, v_s = ([p[i] for p in P] for i in range(7))

    srow = lax.broadcasted_iota(jnp.int32, (SL, SL), 0)
    scol = lax.broadcasted_iota(jnp.int32, (SL, SL), 1)
    same = (srow // L) == (scol // L)
    strict = same & ((srow % L) > (scol % L))
    incl = same & ((srow % L) >= (scol % L))
    eye = jnp.where(srow == scol, 1.0, 0.0)
    n_ab = each(lambda a, b: jnp.where(strict, _mm_nt(a, b), 0.0), at_s, bt_s)
    n_ak = each(lambda a, b: jnp.where(strict, _mm_nt(a, b), 0.0), at_s, kt_s)
    q_rb = each(lambda a, b: jnp.where(incl, _mm_nt(a, b), 0.0), rt_s, bt_s)
    q_rk = each(lambda a, b: jnp.where(incl, _mm_nt(a, b), 0.0), rt_s, kt_s)

    tm = [eye + n for n in n_ab]
    pw = n_ab
    for _ in range(int(math.log2(L)) - 1):
        pw = each(_mm, pw, pw)
        tm = each(lambda t, x: t + _mm(t, x), tm, pw)

    p1 = each(_mm, tm, at_s)
    p2 = each(_mm, tm, each(_mm, n_ak, v_s))
    bg_t = [u.T for u in bg_s]
    kg_t = [u.T for u in kg_s]
    drow = lax.broadcasted_iota(jnp.int32, (GROUP, GROUP), 0)
    dcol = lax.broadcasted_iota(jnp.int32, (GROUP, GROUP), 1)
    m_mat = each(lambda p, bt, x: jnp.where(drow == dcol, jnp.exp(p[7]), 0.0) + _mm3(bt, x), P, bg_t, p1)
    c_mat = each(lambda bt, x, kt, vs: _mm3(bt, x) + _mm3(kt, vs), bg_t, p2, kg_t, v_s)
    yc = each(lambda rt, q, x: rt + _mm(q, x), rt_s, q_rb, p1)
    y0 = each(lambda q, x, qk, vs: _mm(q, x) + _mm(qk, vs), q_rb, p2, q_rk, v_s)

    st = [st_ref[s] for s in seqs]
    y_s = each(lambda a, b, y: _mm3(a, b) + y, yc, st, y0)
    st_new = each(lambda m, b, cm: _mm3(m, b) + cm, m_mat, st, c_mat)
    bd = _head_sum_matrix()
    inv_n = 1.0 / HEAD_DIM
    for s in seqs:
        st_ref[s] = st_new[s]
        y = y_s[s][0:L, :]
        for h in range(1, N_HEADS):
            y = y + y_s[s][h * L:(h + 1) * L, :]
        r, k2, v, g = P[s][8:12]
        mean = _mm2_exact_rhs(y, bd) * inv_n
        yc0 = y - mean
        var = _mm2_exact_rhs(yc0 * yc0, bd) * inv_n
        yn = yc0 * lax.rsqrt(var + RWKV_GN_EPS) * lng_ref[...] + lnb_ref[...]
        bonus = _mm2_exact_rhs(r * k2 * rk_ref[...], bd) * v
        oc_ref[s] = (yn + bonus) * g

    @pl.when(c == pl.num_programs(1) - 1)
    def _():
        sto_ref[...] = st_ref[...]


def _rwkv_prep(s, rw_ref, mu_ref, w0_ref, a0_ref, kkw_ref, ka_ref, w2_ref, a2_ref, g2_ref, prev_ref):
    L = rw_ref.shape[1]
    SL = max(N_HEADS * L, LANES)
    bd = _head_sum_matrix()

    z = rw_ref[s]
    row = lax.broadcasted_iota(jnp.int32, z.shape, 0)
    zprev = jnp.where(row == 0, prev_ref[s], pltpu.roll(z, 1, axis=0))
    prev_ref[s] = z[L - 1:L, :]
    zs = z + (zprev - z) * mu_ref[...]
    r = zs[:, 0:GROUP]
    k = zs[:, GROUP:2 * GROUP]
    v = zs[:, 2 * GROUP:3 * GROUP]
    low = zs[:, 3 * GROUP:3 * GROUP + RWKV_LOW]

    x = -(w0_ref[...] + _mm3(jnp.tanh(low), w2_ref[...]))
    softplus = jnp.maximum(x, 0.0) + jnp.log(1.0 + jnp.exp(-jnp.abs(x)))
    ld = -jnp.exp(-softplus - 0.5)
    a = _sigmoid(a0_ref[...] + _mm3(low, a2_ref[...]))
    g = _mm(_sigmoid(low), g2_ref[...])
    kk = k * kkw_ref[...]
    kk = kk / jnp.maximum(jnp.sqrt(_mm2_exact_rhs(kk * kk, bd)), 1e-12)
    k2 = k * (1.0 + (a - 1.0) * ka_ref[...])

    trow = lax.broadcasted_iota(jnp.int32, (L, L), 0)
    tcol = lax.broadcasted_iota(jnp.int32, (L, L), 1)
    tri = jnp.where(trow >= tcol, 1.0, 0.0).astype(BF16)
    cum = _mm2_exact_rhs_lhs(tri, ld)
    cend = cum[L - 1:L, :]
    g_prev = jnp.exp(cum - ld)
    g_inv = jnp.exp(-cum)
    g_rel = jnp.exp(cend - cum)
    kka = kk * a

    lane_head = _head_of_lane((L, GROUP), 1)

    def stack(u):
        parts = [jnp.where(lane_head == h, u, 0.0) for h in range(N_HEADS)]
        if SL > N_HEADS * L:
            parts.append(jnp.zeros((SL - N_HEADS * L, GROUP), F32))
        return jnp.concatenate(parts, axis=0)

    at_s = stack(-kk * g_prev)
    bt_s = stack(kka * g_inv)
    kt_s = stack(k2 * g_inv)
    rt_s = stack(r * jnp.exp(cum))
    bg_s = stack(kka * g_rel)
    kg_s = stack(k2 * g_rel)
    v_s = stack(v)
    return at_s, bt_s, kt_s, rt_s, bg_s, kg_s, v_s, cend, r, k2, v, g


def _mm2_exact_rhs_lhs(a_bf16, b):
    bh, bl = _split(b)
    d = functools.partial(jnp.dot, preferred_element_type=F32)
    return d(a_bf16, bh) + d(a_bf16, bl)


def _rwkv(rw, shift, st0, p, L, nseq=2):
    b, t, w = rw.shape
    const = lambda i, j: (0, 0)
    vec = pl.BlockSpec((1, GROUP), const)
    low = pl.BlockSpec((RWKV_LOW, GROUP), const)
    return pl.pallas_call(
        _rwkv_kernel,
        grid=(b // nseq, t // L),
        in_specs=[pl.BlockSpec((nseq, L, w), lambda i, j: (i, j, 0)),
                  pl.BlockSpec((nseq, 1, w), lambda i, j: (i, 0, 0)),
                  pl.BlockSpec((nseq, GROUP, GROUP), lambda i, j: (i, 0, 0)),
                  pl.BlockSpec((1, w), const),
                  vec, vec, vec, vec, vec, vec, vec, low, low, low],
        out_specs=[pl.BlockSpec((nseq, L, GROUP), lambda i, j: (i, j, 0)),
                   pl.BlockSpec((nseq, GROUP, GROUP), lambda i, j: (i, 0, 0))],
        out_shape=[jax.ShapeDtypeStruct((b, t, GROUP), F32),
                   jax.ShapeDtypeStruct((b, GROUP, GROUP), F32)],
        scratch_shapes=[pltpu.VMEM((nseq, GROUP, GROUP), F32),
                        pltpu.VMEM((nseq, 1, w), F32)],
        compiler_params=_cparams(2),
        name="rwkv",
    )(rw, shift, st0, p["mu"], p["w0"], p["a0"], p["kkw"], p["ka"], p["rk"], p["lng"], p["lnb"],
      p["w2"], p["a2"], p["g2"])


def _state_to_blockdiag(s):
    b = s.shape[0]
    st = jnp.swapaxes(s, -1, -2)
    eye = jnp.eye(N_HEADS, dtype=s.dtype)
    return jnp.einsum("bhkv,hg->bhkgv", st, eye).reshape(b, GROUP, GROUP)


def _blockdiag_to_state(st):
    b = st.shape[0]
    x = st.reshape(b, N_HEADS, HEAD_DIM, N_HEADS, HEAD_DIM)
    diag = jnp.stack([x[:, h, :, h, :] for h in range(N_HEADS)], axis=1)
    return jnp.swapaxes(diag, -1, -2)


def _outproj_kernel(oa_ref, ob_ref, oc_ref, od_ref, x_ref, wo_ref, g_ref, wq_ref, x1_ref, h_ref, q_ref):
    acc = x_ref[...]
    for i, o_ref in enumerate((oa_ref, ob_ref, oc_ref, od_ref)):
        acc = acc + jnp.dot(o_ref[...].astype(BF16), wo_ref[i], preferred_element_type=F32)
    x1_ref[...] = acc
    ms = jnp.mean(acc * acc, axis=-1, keepdims=True)
    h = (acc * lax.rsqrt(ms + RMS_EPS) * g_ref[...]).astype(BF16)
    h_ref[...] = h
    q_ref[...] = jnp.dot(h, wq_ref[...], preferred_element_type=F32)


def _outproj(oa, ob, oc, od, x, wo, g, wq, tm):
    n, d = x.shape
    dq = wq.shape[1]
    mix = pl.BlockSpec((tm, GROUP), lambda i: (i, 0))
    return pl.pallas_call(
        _outproj_kernel,
        grid=(n // tm,),
        in_specs=[mix, mix, mix, mix,
                  pl.BlockSpec((tm, d), lambda i: (i, 0)),
                  pl.BlockSpec((4, GROUP, d), lambda i: (0, 0, 0)),
                  pl.BlockSpec((1, d), lambda i: (0, 0)),
                  pl.BlockSpec((d, dq), lambda i: (0, 0))],
        out_specs=[pl.BlockSpec((tm, d), lambda i: (i, 0)),
                   pl.BlockSpec((tm, d), lambda i: (i, 0)),
                   pl.BlockSpec((tm, dq), lambda i: (i, 0))],
        out_shape=[jax.ShapeDtypeStruct((n, d), F32),
                   jax.ShapeDtypeStruct((n, d), BF16),
                   jax.ShapeDtypeStruct((n, dq), F32)],
        compiler_params=_cparams(1),
        name="outproj",
    )(oa, ob, oc, od, x, wo, g, wq)


_CAND_PIECES = ((0, 0, 8), (0, 8, 8), (1, 0, 8), (2, 0, 5), (3, 0, 4), (4, 0, 3), (5, 0, 2),
                (6, 0, 2), (7, 0, 2))
_CAND_ROWS = 8 * (len(_CAND_PIECES) + 1)


def _sort_network(n):
    size = 1
    while size < n:
        size *= 2
    pairs = []
    p = 1
    while p < size:
        k = p
        while k >= 1:
            for j in range(k % p, size - k, 2 * k):
                for i in range(min(k, size - j - k)):
                    if (i + j) // (2 * p) == (i + j + k) // (2 * p):
                        pairs.append((i + j, i + j + k))
            k //= 2
        p *= 2
    return tuple((i, j) for i, j in pairs if j < n)


_SUBLANES = 8


def _top_values(x, depth):
    groups = x.shape[0] // _SUBLANES
    v = [x[_SUBLANES * k:_SUBLANES * (k + 1), :] for k in range(groups)]
    for i, j in _sort_network(groups):
        v[i], v[j] = jnp.maximum(v[i], v[j]), jnp.minimum(v[i], v[j])
    sidx = lax.broadcasted_iota(jnp.int32, v[0].shape, 0).astype(F32)
    out = []
    for r in range(depth):
        mx = jnp.max(v[0], axis=0, keepdims=True)
        first = jnp.min(jnp.where(v[0] == mx, sidx, float(_SUBLANES)), axis=0, keepdims=True)
        pop = sidx == first
        out.append(mx)
        reachable = depth - 1 - r
        for k in range(min(groups - 1, reachable)):
            v[k] = jnp.where(pop, v[k + 1], v[k])
        if reachable > groups - 1:
            v[groups - 1] = jnp.where(pop, -jnp.inf, v[groups - 1])
    return jnp.concatenate(out, axis=0)


_SELECT_TILES_PER_ITER = 2


def _peer_select_kernel(q_ref, k1_ref, k2_ref, s1_ref, s2_ref, thr_ref, stat_ref):
    tn = q_ref.shape[0]
    n_lt = tn // LANES
    q = q_ref[...].astype(BF16)
    nt = functools.partial(lax.dot_general, dimension_numbers=(((1,), (1,)), ((), ())),
                           preferred_element_type=F32)
    s1_all = nt(k1_ref[...], q)
    s2_all = nt(k2_ref[...], q)
    for lt in range(n_lt):
        s1_ref[lt, 0] = s1_all[:, lt * LANES:(lt + 1) * LANES]
        s2_ref[lt, 0] = s2_all[:, lt * LANES:(lt + 1) * LANES]

    sub = lax.broadcasted_iota(jnp.int32, (8, LANES), 0)

    def one_tile(lt):
        s1 = s1_ref[lt, 0]
        v1 = _top_values(s1, PEER_TOPK)
        v2 = _top_values(s2_ref[lt, 0], PEER_TOPK)
        pieces = [jnp.where(sub < nb, v1[a:a + 1, :] + v2[b0:b0 + 8, :], -jnp.inf)
                  for a, b0, nb in _CAND_PIECES]
        pieces.append(v1[8:16, :] + v2[0:1, :])
        cv = _top_values(jnp.concatenate(pieces, axis=0), PEER_TOPK)
        tau = cv[PEER_TOPK - 1:PEER_TOPK, :]
        z = jnp.sum(jnp.exp(cv - cv[0:1, :]), axis=0, keepdims=True)
        stat_ref[lt, 0] = jnp.concatenate(
            [tau, v1[0:1, :], v2[0:1, :], 1.0 / z, jnp.zeros((4, LANES), F32)], axis=0)
        thr = jnp.full((PEER_NKEYS, LANES), jnp.inf, F32)
        for b in range(PEER_TOPK):
            vb = v2[b:b + 1, :]
            thr = jnp.where((s1 + vb) >= tau, vb, thr)
        thr_ref[lt, 0] = thr

    per_iter = math.gcd(n_lt, _SELECT_TILES_PER_ITER)

    def tile_body(i, carry):
        for u in range(per_iter):
            one_tile(i * per_iter + u)
        return carry

    lax.fori_loop(0, n_lt // per_iter, tile_body, 0)


def _peer_select(q, k1p, k2p, tn):
    n, dq = q.shape
    qd = dq // PEER_HEADS
    n_lt = tn // LANES
    keyed = pl.BlockSpec((n_lt, 1, PEER_NKEYS, LANES), lambda i, h: (i, h, 0, 0))
    keyed_shape = jax.ShapeDtypeStruct((n // LANES, PEER_HEADS, PEER_NKEYS, LANES), F32)
    return pl.pallas_call(
        _peer_select_kernel,
        grid=(n // tn, PEER_HEADS),
        in_specs=[pl.BlockSpec((tn, qd), lambda i, h: (i, h)),
                  pl.BlockSpec((PEER_NKEYS, qd), lambda i, h: (0, 0)),
                  pl.BlockSpec((PEER_NKEYS, qd), lambda i, h: (0, 0))],
        out_specs=[keyed, keyed, keyed,
                   pl.BlockSpec((n_lt, 1, 8, LANES), lambda i, h: (i, h, 0, 0))],
        out_shape=[keyed_shape, keyed_shape, keyed_shape,
                   jax.ShapeDtypeStruct((n // LANES, PEER_HEADS, 8, LANES), F32)],
        compiler_params=_cparams(2),
        name="peer_select",
    )(q, k1p, k2p)


def _peer_dense_kernel(h_ref, u_ref, vt_ref, s1_ref, s2_ref, thr_ref, stat_ref, x1_ref, y_ref,
                       acc_ref, e1_ref, e2_ref, act_ref, w_ref):
    j = pl.program_id(1)
    nj = pl.num_programs(1)
    eb = u_ref.shape[0]
    n_lt = s2_ref.shape[0]
    ne1 = eb // PEER_NKEYS

    @pl.when(j == 0)
    def _():
        acc_ref[...] = jnp.zeros(acc_ref.shape, F32)
        for lt in range(n_lt):
            for h in range(PEER_HEADS):
                st = stat_ref[lt, h]
                e1_ref[lt, h] = jnp.exp(s1_ref[lt, h] - st[1:2, :]) * st[3:4, :]
                e2_ref[lt, h] = jnp.exp(s2_ref[lt, h] - st[2:3, :])

    act = lax.dot_general(u_ref[...], h_ref[...], (((1,), (1,)), ((), ())),
                          preferred_element_type=F32)
    for lt in range(n_lt):
        act_ref[lt] = act[:, lt * LANES:(lt + 1) * LANES]

    def first_key(e, carry):
        e1 = j * ne1 + e
        r0 = pl.multiple_of(e * PEER_NKEYS, PEER_NKEYS)

        def token_tile(lt, carry):
            gate = jnp.zeros((PEER_NKEYS, LANES), F32)
            for h in range(PEER_HEADS):
                chosen = s2_ref[lt, h] >= thr_ref[lt, h, pl.ds(e1, 1), :]
                gate = gate + jnp.where(chosen, e2_ref[lt, h], 0.0) * e1_ref[lt, h, pl.ds(e1, 1), :]
            a = act_ref[lt, pl.ds(r0, PEER_NKEYS), :]
            gelu = 0.5 * a * (1.0 + lax.erf(a * (2.0 ** -0.5)))
            w_ref[lt, pl.ds(r0, PEER_NKEYS), :] = (gate * gelu).astype(BF16)
            return carry

        return lax.fori_loop(0, n_lt, token_tile, carry)

    lax.fori_loop(0, ne1, first_key, 0)
    w = jnp.concatenate([w_ref[lt] for lt in range(n_lt)], axis=1)
    acc_ref[...] += jnp.dot(vt_ref[...], w, preferred_element_type=F32)

    @pl.when(j == nj - 1)
    def _():
        y_ref[...] = x1_ref[...] + acc_ref[...].T


def _peer_dense(h, u, vt, s1, s2, thr, stat, x1, tn, eb):
    n, d = x1.shape
    ne = u.shape[0]
    n_lt = tn // LANES
    keyed = pl.BlockSpec((n_lt, PEER_HEADS, PEER_NKEYS, LANES), lambda i, j: (i, 0, 0, 0))
    return pl.pallas_call(
        _peer_dense_kernel,
        grid=(n // tn, ne // eb),
        in_specs=[pl.BlockSpec((tn, d), lambda i, j: (i, 0)),
                  pl.BlockSpec((eb, d), lambda i, j: (j, 0)),
                  pl.BlockSpec((d, eb), lambda i, j: (0, j)),
                  keyed, keyed, keyed,
                  pl.BlockSpec((n_lt, PEER_HEADS, 8, LANES), lambda i, j: (i, 0, 0, 0)),
                  pl.BlockSpec((tn, d), lambda i, j: (i, 0))],
        out_specs=[pl.BlockSpec((tn, d), lambda i, j: (i, 0))],
        out_shape=[jax.ShapeDtypeStruct((n, d), F32)],
        scratch_shapes=[pltpu.VMEM((d, tn), F32),
                        pltpu.VMEM((n_lt, PEER_HEADS, PEER_NKEYS, LANES), F32),
                        pltpu.VMEM((n_lt, PEER_HEADS, PEER_NKEYS, LANES), F32),
                        pltpu.VMEM((n_lt, eb, LANES), F32),
                        pltpu.VMEM((n_lt, eb, LANES), BF16)],
        compiler_params=_cparams(2),
        name="peer_dense",
    )(h, u, vt, s1, s2, thr, stat, x1)[0]


def _rope_tables(pos):
    half = HEAD_DIM // 2
    inv_freq = ROPE_THETA ** (-jnp.arange(half, dtype=F32) / half)
    ang = pos.astype(F32)[:, None] * inv_freq[None, :]
    cos = jnp.tile(jnp.cos(ang), (1, 2 * N_HEADS))
    sin = jnp.sin(ang)
    sin_signed = jnp.tile(jnp.concatenate([-sin, sin], axis=1), (1, N_HEADS))
    return cos, sin_signed


def _row_tile(n, pref):
    return pref if n % pref == 0 else n


def _seq_group(b, pref):
    return math.gcd(b, pref)


def _layer_params(l, norm_mix, w_in, q_norm, k_norm, conf_dw, conf_dw_b, conf_ln_g, conf_ln_b,
                  rwkv_mu, rwkv_w0, rwkv_w2, rwkv_a0, rwkv_a2, rwkv_g2, rwkv_kk, rwkv_ka, rwkv_rk,
                  rwkv_ln_g, rwkv_ln_b, sc_w, w_out, norm_ffn, peer_wq, peer_k1, peer_k2, peer_u, peer_v):
    row = lambda a: a[l].reshape(1, -1)
    att_c = 3 * GROUP
    conf_c = 2 * GROUP
    rw_c = rwkv_mu.shape[1]
    wi = w_in[l].astype(BF16)
    bounds = [0, att_c, att_c + conf_c, att_c + conf_c + rw_c, wi.shape[1]]
    rd, ra, rg = rwkv_w2.shape[1], rwkv_a2.shape[1], rwkv_g2.shape[1]
    zeros = lambda r: jnp.zeros((r, GROUP), F32)
    half = peer_k1.shape[2]
    return dict(
        norm_mix=row(norm_mix),
        w_in=[wi[:, bounds[i]:bounds[i + 1]] for i in range(4)],
        qg=jnp.tile(row(q_norm), (1, N_HEADS)), kg=jnp.tile(row(k_norm), (1, N_HEADS)),
        dw=conf_dw[l], dwb=row(conf_dw_b), clng=row(conf_ln_g), clnb=row(conf_ln_b), scw=sc_w[l],
        rwkv=dict(mu=row(rwkv_mu), w0=row(rwkv_w0), a0=row(rwkv_a0), kkw=row(rwkv_kk), ka=row(rwkv_ka),
                  rk=row(rwkv_rk), lng=row(rwkv_ln_g), lnb=row(rwkv_ln_b),
                  w2=jnp.concatenate([rwkv_w2[l], zeros(ra + rg)], axis=0),
                  a2=jnp.concatenate([zeros(rd), rwkv_a2[l], zeros(rg)], axis=0),
                  g2=jnp.concatenate([zeros(rd + ra), rwkv_g2[l]], axis=0)),
        w_out=w_out[l].astype(BF16).reshape(4, GROUP, -1),
        norm_ffn=row(norm_ffn), wq=peer_wq[l].astype(BF16),
        k1p=jnp.concatenate([peer_k1[l], jnp.zeros_like(peer_k1[l])], axis=1).astype(BF16),
        k2p=jnp.concatenate([jnp.zeros_like(peer_k2[l]), peer_k2[l]], axis=1).astype(BF16),
        u=peer_u[l].astype(BF16), vt=peer_v[l].astype(BF16).T,
    )


def _mix_and_ffn(x, oa, ob, oc, od, p):
    b, t, d = x.shape
    n = b * t
    flat = lambda a: a.reshape(n, a.shape[-1])
    tm = _row_tile(n, 512)
    x1, h2, q = _outproj(flat(oa), flat(ob), flat(oc), flat(od), flat(x), p["w_out"], p["norm_ffn"],
                         p["wq"], tm)
    tn = _row_tile(n, 512)
    s1, s2, thr, stat = _peer_select(q, p["k1p"], p["k2p"], tn)
    y = _peer_dense(h2, p["u"], p["vt"], s1, s2, thr, stat, x1, tn, 1024)
    return y.reshape(b, t, d)


def _common_mixers(x, p, conf_buf, shift_prev, wkv_prev, sc_buf, chunk, nseq, conv_tile):
    b, t, d = x.shape
    n = b * t
    att, conf, rw, sc = _proj(x.reshape(n, d), p["norm_mix"], p["w_in"], _row_tile(n, 512))
    att = att.reshape(b, t, -1)
    conf = conf.reshape(b, t, -1)
    rw = rw.reshape(b, t, -1)
    sc = sc.reshape(b, t, -1)
    ob, od, conf_new, sc_new = _conv(conf, sc, conf_buf, sc_buf, p["dw"], p["dwb"], p["clng"], p["clnb"],
                                     p["scw"], conv_tile)
    oc, st_out = _rwkv(rw, shift_prev[:, None, :], _state_to_blockdiag(wkv_prev), p["rwkv"], chunk, nseq)
    return att, ob, oc, od, conf_new, rw[:, -1, :], _blockdiag_to_state(st_out), sc_new


def kernel(x_prompt, x_sample, cache_k, cache_v, page_table, state_conf_conv, state_rwkv_shift, state_rwkv_wkv, state_short_conv, norm_mix, w_in, q_norm, k_norm, conf_dw, conf_dw_b, conf_ln_g, conf_ln_b, rwkv_mu, rwkv_w0, rwkv_w2, rwkv_a0, rwkv_a2, rwkv_g2, rwkv_kk, rwkv_ka, rwkv_rk, rwkv_ln_g, rwkv_ln_b, sc_w, w_out, norm_ffn, peer_wq, peer_k1, peer_k2, peer_u, peer_v):
    depth = w_in.shape[0]
    bp, seq, _ = x_prompt.shape
    db, dec_seq, _ = x_sample.shape
    n_pool = cache_k.shape[1]
    past_len = page_table.shape[1] * PAGE
    ck = jnp.transpose(cache_k, (0, 1, 3, 4, 2))
    cv = jnp.transpose(cache_v, (0, 1, 3, 4, 2))
    cos_p, sin_p = _rope_tables(jnp.arange(seq))
    cos_s, sin_s = _rope_tables(past_len + jnp.arange(dec_seq))
    dt = x_prompt.dtype

    yp, ys = x_prompt, x_sample
    outs_p = [[] for _ in range(6)]
    outs_s = [[] for _ in range(6)]
    for l in range(depth):
        p = _layer_params(l, norm_mix, w_in, q_norm, k_norm, conf_dw, conf_dw_b, conf_ln_g, conf_ln_b,
                          rwkv_mu, rwkv_w0, rwkv_w2, rwkv_a0, rwkv_a2, rwkv_g2, rwkv_kk, rwkv_ka,
                          rwkv_rk, rwkv_ln_g, rwkv_ln_b, sc_w, w_out, norm_ffn, peer_wq, peer_k1,
                          peer_k2, peer_u, peer_v)
        att, ob, oc, od, conf_new, shift_new, wkv_new, sc_new = _common_mixers(
            yp, p,
            jnp.zeros((bp, CONF_TAPS - 1, GROUP), dt),
            jnp.zeros((bp, rwkv_mu.shape[1]), dt),
            jnp.zeros((bp, N_HEADS, HEAD_DIM, HEAD_DIM), dt),
            jnp.zeros((bp, SC_TAPS - 1, GROUP), dt),
            chunk=64, nseq=_seq_group(bp, 4), conv_tile=_row_tile(seq, 512))
        oa, k_rot = _moba_prompt(att, cos_p, sin_p, p["qg"], p["kg"])
        new_p = (k_rot.reshape(bp, seq, N_HEADS, HEAD_DIM),
                 att[:, :, 2 * GROUP:].reshape(bp, seq, N_HEADS, HEAD_DIM),
                 conf_new, shift_new, wkv_new, sc_new)
        yp = _mix_and_ffn(yp, oa, ob, oc, od, p)

        att, ob, oc, od, conf_new, shift_new, wkv_new, sc_new = _common_mixers(
            ys, p, state_conf_conv[l], state_rwkv_shift[l], state_rwkv_wkv[l], state_short_conv[l],
            chunk=dec_seq, nseq=_seq_group(db, 8), conv_tile=dec_seq)
        sel, q_rot, k_rot = _decode_sel(att, cos_s, sin_s, p["qg"], p["kg"], ck, page_table, l)
        v_new = att[:, :, 2 * GROUP:]
        oa = _decode_att(sel, q_rot, k_rot, v_new, ck, cv, page_table, l)
        new_s = (k_rot.reshape(db, dec_seq, N_HEADS, HEAD_DIM),
                 v_new.reshape(db, dec_seq, N_HEADS, HEAD_DIM),
                 conf_new, shift_new, wkv_new, sc_new)
        ys = _mix_and_ffn(ys, oa, ob, oc, od, p)

        for lst, a in zip(outs_p, new_p):
            lst.append(a)
        for lst, a in zip(outs_s, new_s):
            lst.append(a)
    pk, pv, pconf, pshift, pwkv, psc = [jnp.stack(a) for a in outs_p]
    sk, sv, sconf, sshift, swkv, ssc = [jnp.stack(a) for a in outs_s]
    return (yp, ys, pk, pv, sk, sv, pconf, sconf, pshift, sshift, pwkv, swkv, psc, ssc)
```

```python
import functools
import math

import jax
import jax.numpy as jnp
from jax import lax
from jax.experimental import pallas as pl
from jax.experimental.pallas import tpu as pltpu

F32 = jnp.float32
BF16 = jnp.bfloat16

HEAD_DIM = 64
GROUP = 256
N_HEADS = GROUP // HEAD_DIM
MOBA_BLOCK = 256
MOBA_TOPK = 3
ROPE_THETA = 10000.0
PAGE = 128
CONF_TAPS = 31
SC_TAPS = 3
RWKV_LOW = 128
PEER_HEADS = 8
PEER_NKEYS = 128
PEER_TOPK = 16
PEER_EXPERT_BLOCK = 1024
RMS_EPS = 1e-6
LN_EPS = 1e-5
RWKV_GN_EPS = 64e-5
NEG_INF = -1e30
LANES = 128
VMEM_LIMIT = 56 * 1024 * 1024

_ARB = "arbitrary"


def _cparams(n_axes, vmem=VMEM_LIMIT):
    return pltpu.CompilerParams(dimension_semantics=(_ARB,) * n_axes, vmem_limit_bytes=vmem)


def _mm(a, b):
    return jnp.dot(a.astype(BF16), b.astype(BF16), preferred_element_type=F32)


def _mm_nt(a, b):
    return lax.dot_general(a.astype(BF16), b.astype(BF16), (((1,), (1,)), ((), ())),
                           preferred_element_type=F32)


def _split(a):
    hi = a.astype(BF16)
    lo = (a - hi.astype(F32)).astype(BF16)
    return hi, lo


def _mm3(a, b):
    ah, al = _split(a)
    bh, bl = _split(b)
    d = functools.partial(jnp.dot, preferred_element_type=F32)
    return d(ah, bh) + d(ah, bl) + d(al, bh)


def _mm3_nt(a, b):
    ah, al = _split(a)
    bh, bl = _split(b)
    d = functools.partial(lax.dot_general, dimension_numbers=(((1,), (1,)), ((), ())),
                          preferred_element_type=F32)
    return d(ah, bh) + d(ah, bl) + d(al, bh)


def _mm2_exact_rhs(a, b_bf16):
    ah, al = _split(a)
    d = functools.partial(jnp.dot, preferred_element_type=F32)
    return d(ah, b_bf16) + d(al, b_bf16)


def _sigmoid(x):
    return 1.0 / (1.0 + jnp.exp(-x))


def _head_of_lane(shape, axis):
    return lax.broadcasted_iota(jnp.int32, shape, axis) // HEAD_DIM


def _head_sum_matrix():
    r = _head_of_lane((GROUP, GROUP), 0)
    c = _head_of_lane((GROUP, GROUP), 1)
    return jnp.where(r == c, 1.0, 0.0).astype(BF16)


def _proj_kernel(x_ref, g_ref, wa_ref, wc_ref, wr_ref, ws_ref, oa_ref, oc_ref, or_ref, os_ref):
    x = x_ref[...]
    ms = jnp.mean(x * x, axis=-1, keepdims=True)
    h = (x * lax.rsqrt(ms + RMS_EPS) * g_ref[...]).astype(BF16)
    for w_ref, o_ref in ((wa_ref, oa_ref), (wc_ref, oc_ref), (wr_ref, or_ref), (ws_ref, os_ref)):
        o_ref[...] = jnp.dot(h, w_ref[...], preferred_element_type=F32)


def _proj(x, g, ws, tm):
    n, d = x.shape
    widths = [w.shape[1] for w in ws]
    return pl.pallas_call(
        _proj_kernel,
        grid=(n // tm,),
        in_specs=[pl.BlockSpec((tm, d), lambda i: (i, 0)),
                  pl.BlockSpec((1, d), lambda i: (0, 0))]
                 + [pl.BlockSpec((d, wd), lambda i: (0, 0)) for wd in widths],
        out_specs=[pl.BlockSpec((tm, wd), lambda i: (i, 0)) for wd in widths],
        out_shape=[jax.ShapeDtypeStruct((n, wd), F32) for wd in widths],
        compiler_params=_cparams(1),
        name="proj",
    )(x, g, *ws)


def _head_rms(x, gain, bd):
    ms = _mm2_exact_rhs(x * x, bd) * (1.0 / HEAD_DIM)
    return x * lax.rsqrt(ms + RMS_EPS) * gain


def _rope(x, cos, sin_signed):
    lane = lax.broadcasted_iota(jnp.int32, x.shape, 1) % HEAD_DIM
    lo = pltpu.roll(x, GROUP - HEAD_DIM // 2, axis=1)
    hi = pltpu.roll(x, HEAD_DIM // 2, axis=1)
    rot = jnp.where(lane < HEAD_DIM // 2, lo, hi)
    return x * cos + rot * sin_signed


def _top_rows(g, n_rows, k):
    ridx = lax.broadcasted_iota(jnp.int32, g.shape, 0).astype(F32)
    sel = jnp.zeros(g.shape, F32)
    cur = g
    for _ in range(k):
        mx = jnp.max(cur, axis=0, keepdims=True)
        first = jnp.min(jnp.where(cur == mx, ridx, float(n_rows)), axis=0, keepdims=True)
        hit = ridx == first
        sel = jnp.where(hit, 1.0, sel)
        cur = jnp.where(hit, -jnp.inf, cur)
    return sel


def _moba_prompt_kernel(att_ref, cos_ref, sin_ref, qg_ref, kg_ref, oa_ref, kr_ref,
                        kb_ref, vt_ref, km_ref, sel_ref, qm_ref):
    qi = pl.program_id(1)
    nblk = kb_ref.shape[0]
    blk = MOBA_BLOCK
    bd = _head_sum_matrix()

    @pl.when(qi == 0)
    def _():
        km_ref[...] = jnp.zeros(km_ref.shape, F32)

    a = att_ref[0]
    cos = cos_ref[...]
    sin = sin_ref[...]
    qr = _rope(_head_rms(a[:, 0:GROUP], qg_ref[...], bd), cos, sin)
    kr = _rope(_head_rms(a[:, GROUP:2 * GROUP], kg_ref[...], bd), cos, sin)
    v = a[:, 2 * GROUP:3 * GROUP]
    kr_ref[0] = kr
    kb_ref[qi] = kr.astype(BF16)
    vt_ref[qi] = v.T.astype(BF16)
    km_ref[pl.ds(qi, 1), :] = jnp.mean(kr, axis=0, keepdims=True)

    qs = qr * (HEAD_DIM ** -0.5)
    lane_head = _head_of_lane((blk, GROUP), 1)
    km = km_ref[...]
    km_head = _head_of_lane(km.shape, 1)
    blk_row = lax.broadcasted_iota(jnp.int32, (nblk, blk), 0)
    krow = lax.broadcasted_iota(jnp.int32, (blk, blk), 0)
    qcol = lax.broadcasted_iota(jnp.int32, (blk, blk), 1)
    kcur = kb_ref[qi]
    heads = range(N_HEADS)
    head_rows = [slice(h * HEAD_DIM, (h + 1) * HEAD_DIM) for h in heads]
    nt = functools.partial(lax.dot_general, dimension_numbers=(((1,), (1,)), ((), ())),
                           preferred_element_type=F32)
    qms = [jnp.where(lane_head == h, qs, 0.0) for h in heads]
    for h in heads:
        qm_ref[h] = qms[h].astype(BF16)
    past = blk_row < qi
    gates = [_mm3_nt(jnp.where(km_head == h, km, 0.0), qms[h]) for h in heads]
    for h in heads:
        sel = _top_rows(jnp.where(past, gates[h], NEG_INF), nblk, MOBA_TOPK)
        sel_ref[h] = jnp.where(past, sel, 0.0)
    sts = [jnp.where(krow <= qcol, nt(kcur, qm_ref[h]), NEG_INF) for h in heads]
    ms = [jnp.max(st, axis=0, keepdims=True) for st in sts]
    ps = [jnp.exp(st - m) for st, m in zip(sts, ms)]
    ls = [jnp.sum(p, axis=0, keepdims=True) for p in ps]
    accs = [jnp.dot(vt_ref[qi, head_rows[h], :], ps[h].astype(BF16), preferred_element_type=F32)
            for h in heads]

    def past_block(j, carry):
        ms, ls, accs = carry
        kj = kb_ref[j]
        sts = [jnp.where(sel_ref[h, pl.ds(j, 1), :] > 0.0, nt(kj, qm_ref[h]), NEG_INF) for h in heads]
        m_new = [jnp.maximum(m, jnp.max(st, axis=0, keepdims=True)) for m, st in zip(ms, sts)]
        alpha = [jnp.exp(m - mn) for m, mn in zip(ms, m_new)]
        ps = [jnp.exp(st - mn) for st, mn in zip(sts, m_new)]
        ls = [a * l + jnp.sum(p, axis=0, keepdims=True) for a, l, p in zip(alpha, ls, ps)]
        accs = [alpha[h] * accs[h] + jnp.dot(vt_ref[j, head_rows[h], :], ps[h].astype(BF16),
                                             preferred_element_type=F32) for h in heads]
        return m_new, ls, accs

    ms, ls, accs = lax.fori_loop(0, qi, past_block, (ms, ls, accs))
    oa_ref[0] = jnp.concatenate([acc / l for acc, l in zip(accs, ls)], axis=0).T


def _moba_prompt(att, cos, sin, qg, kg):
    b, t, _ = att.shape
    nblk = t // MOBA_BLOCK
    blk = MOBA_BLOCK
    return pl.pallas_call(
        _moba_prompt_kernel,
        grid=(b, nblk),
        in_specs=[pl.BlockSpec((1, blk, 3 * GROUP), lambda i, j: (i, j, 0)),
                  pl.BlockSpec((blk, GROUP), lambda i, j: (j, 0)),
                  pl.BlockSpec((blk, GROUP), lambda i, j: (j, 0)),
                  pl.BlockSpec((1, GROUP), lambda i, j: (0, 0)),
                  pl.BlockSpec((1, GROUP), lambda i, j: (0, 0))],
        out_specs=[pl.BlockSpec((1, blk, GROUP), lambda i, j: (i, j, 0)),
                   pl.BlockSpec((1, blk, GROUP), lambda i, j: (i, j, 0))],
        out_shape=[jax.ShapeDtypeStruct((b, t, GROUP), F32),
                   jax.ShapeDtypeStruct((b, t, GROUP), F32)],
        scratch_shapes=[pltpu.VMEM((nblk, blk, GROUP), BF16),
                        pltpu.VMEM((nblk, GROUP, blk), BF16),
                        pltpu.VMEM((nblk, GROUP), F32),
                        pltpu.VMEM((N_HEADS, nblk, blk), F32),
                        pltpu.VMEM((N_HEADS, blk, GROUP), BF16)],
        compiler_params=_cparams(2),
        name="moba_prompt",
    )(att, cos, sin, qg, kg)


def _stack_heads(x):
    lane_head = _head_of_lane(x.shape, 1)
    return jnp.concatenate([jnp.where(lane_head == h, x, 0.0) for h in range(N_HEADS)], axis=0)


def _decode_sel_kernel(pt_ref, att_ref, cos_ref, sin_ref, qg_ref, kg_ref, *rest, pages_per_step):
    page_refs = rest[:pages_per_step]
    sel_ref, qr_ref, kr_ref, q2_ref, gate_ref = rest[pages_per_step:]
    s = pl.program_id(1)
    n_steps = pl.num_programs(1)
    blocks_per_step = pages_per_step * PAGE // MOBA_BLOCK
    n_past = n_steps * blocks_per_step
    pages_per_block = MOBA_BLOCK // PAGE
    rows = gate_ref.shape[0]
    bd = _head_sum_matrix()

    @pl.when(s == 0)
    def _():
        gate_ref[...] = jnp.zeros(gate_ref.shape, F32)
        a = att_ref[0]
        qr = _rope(_head_rms(a[:, 0:GROUP], qg_ref[...], bd), cos_ref[...], sin_ref[...])
        qr_ref[0] = qr
        kr_ref[0] = _rope(_head_rms(a[:, GROUP:2 * GROUP], kg_ref[...], bd), cos_ref[...], sin_ref[...])
        hi, lo = _split(_stack_heads(qr * (HEAD_DIM ** -0.5)))
        q2_ref[...] = jnp.concatenate([hi, lo], axis=0)

    lane = lax.broadcasted_iota(jnp.int32, gate_ref.shape, 1)
    for bi in range(blocks_per_step):
        tot = jnp.zeros((rows, 1), F32)
        for pi in range(pages_per_block):
            kt = page_refs[bi * pages_per_block + pi][0, 0].reshape(GROUP, PAGE).astype(BF16)
            sc = jnp.dot(q2_ref[...], kt, preferred_element_type=F32)
            tot = tot + jnp.sum(sc[0:rows, :] + sc[rows:2 * rows, :], axis=1, keepdims=True)
        gate_ref[...] += jnp.where(lane == s * blocks_per_step + bi, tot * (1.0 / MOBA_BLOCK), 0.0)

    @pl.when(s == n_steps - 1)
    def _():
        gate = gate_ref[...]
        past = lane < n_past
        cur = jnp.where(past, gate, NEG_INF)
        lidx = lane.astype(F32)
        sel = jnp.zeros(gate.shape, F32)
        for _ in range(MOBA_TOPK):
            mx = jnp.max(cur, axis=1, keepdims=True)
            first = jnp.min(jnp.where(cur == mx, lidx, float(LANES)), axis=1, keepdims=True)
            hit = lidx == first
            sel = jnp.where(hit, 1.0, sel)
            cur = jnp.where(hit, -jnp.inf, cur)
        sel_ref[0] = jnp.where(past, sel, 0.0)


def _decode_sel(att, cos, sin, qg, kg, cache_kt, page_table, layer, pages_per_step=8):
    db, t, _ = att.shape
    n_pages = page_table.shape[1]
    n_steps = n_pages // pages_per_step
    rows = N_HEADS * t
    assert n_pages * PAGE // MOBA_BLOCK <= LANES

    def page_map(i):
        return lambda b, s, pt: (layer, pt[b, s * pages_per_step + i], 0, 0, 0)

    grid_spec = pltpu.PrefetchScalarGridSpec(
        num_scalar_prefetch=1,
        grid=(db, n_steps),
        in_specs=[pl.BlockSpec((1, t, 3 * GROUP), lambda b, s, pt: (b, 0, 0)),
                  pl.BlockSpec((t, GROUP), lambda b, s, pt: (0, 0)),
                  pl.BlockSpec((t, GROUP), lambda b, s, pt: (0, 0)),
                  pl.BlockSpec((1, GROUP), lambda b, s, pt: (0, 0)),
                  pl.BlockSpec((1, GROUP), lambda b, s, pt: (0, 0))]
                 + [pl.BlockSpec((1, 1, N_HEADS, HEAD_DIM, PAGE), page_map(i)) for i in range(pages_per_step)],
        out_specs=[pl.BlockSpec((1, rows, LANES), lambda b, s, pt: (b, 0, 0)),
                   pl.BlockSpec((1, t, GROUP), lambda b, s, pt: (b, 0, 0)),
                   pl.BlockSpec((1, t, GROUP), lambda b, s, pt: (b, 0, 0))],
        scratch_shapes=[pltpu.VMEM((2 * rows, GROUP), BF16),
                        pltpu.VMEM((rows, LANES), F32)],
    )
    return pl.pallas_call(
        functools.partial(_decode_sel_kernel, pages_per_step=pages_per_step),
        grid_spec=grid_spec,
        out_shape=[jax.ShapeDtypeStruct((db, N_HEADS * t, LANES), F32),
                   jax.ShapeDtypeStruct((db, t, GROUP), F32),
                   jax.ShapeDtypeStruct((db, t, GROUP), F32)],
        compiler_params=_cparams(2),
        name="decode_sel",
    )(page_table, att, cos, sin, qg, kg, *([cache_kt] * pages_per_step))


def _decode_att_kernel(pt_ref, sel_ref, qr_ref, kn_ref, vn_ref, *rest, pages_per_step):
    k_refs = rest[:pages_per_step]
    v_refs = rest[pages_per_step:2 * pages_per_step]
    oa_ref, qbd_ref, acc_ref, m_ref, l_ref = rest[2 * pages_per_step:]
    j = pl.program_id(1)
    n_steps = pl.num_programs(1)
    t = qr_ref.shape[1]
    blocks_per_step = pages_per_step * PAGE // MOBA_BLOCK

    @pl.when(j == 0)
    def _():
        qbd_ref[...] = _stack_heads(qr_ref[0] * (HEAD_DIM ** -0.5)).astype(BF16)
        pad = jnp.zeros((MOBA_BLOCK - t, GROUP), F32)
        kn = jnp.concatenate([kn_ref[0], pad], axis=0).astype(BF16)
        vn = jnp.concatenate([vn_ref[0], pad], axis=0).astype(BF16)
        st = lax.dot_general(qbd_ref[...], kn, (((1,), (1,)), ((), ())), preferred_element_type=F32)
        qpos = lax.broadcasted_iota(jnp.int32, st.shape, 0) % t
        kpos = lax.broadcasted_iota(jnp.int32, st.shape, 1)
        st = jnp.where(kpos <= qpos, st, NEG_INF)
        m = jnp.max(st, axis=1, keepdims=True)
        p = jnp.exp(st - m)
        m_ref[...] = m
        l_ref[...] = jnp.sum(p, axis=1, keepdims=True)
        acc_ref[...] = jnp.dot(p.astype(BF16), vn, preferred_element_type=F32)

    qbd = qbd_ref[...]
    pages_per_block = MOBA_BLOCK // PAGE
    sel = sel_ref[0]
    lane = lax.broadcasted_iota(jnp.int32, sel.shape, 1)
    masked = []
    for pi, r in enumerate(k_refs):
        sc = jnp.dot(qbd, r[0, 0].reshape(GROUP, PAGE).astype(BF16), preferred_element_type=F32)
        blk = j * blocks_per_step + pi // pages_per_block
        chosen = jnp.sum(jnp.where(lane == blk, sel, 0.0), axis=1, keepdims=True) > 0.0
        masked.append(jnp.where(chosen, sc, NEG_INF))
    st = jnp.concatenate(masked, axis=1)
    m_old = m_ref[...]
    m_new = jnp.maximum(m_old, jnp.max(st, axis=1, keepdims=True))
    alpha = jnp.exp(m_old - m_new)
    p32 = jnp.exp(st - m_new)
    p = p32.astype(BF16)
    m_ref[...] = m_new
    l_ref[...] = alpha * l_ref[...] + jnp.sum(p32, axis=1, keepdims=True)
    acc = alpha * acc_ref[...]
    for pi, r in enumerate(v_refs):
        acc = acc + lax.dot_general(p[:, pi * PAGE:(pi + 1) * PAGE], r[0, 0].reshape(GROUP, PAGE).astype(BF16),
                                    (((1,), (1,)), ((), ())), preferred_element_type=F32)
    acc_ref[...] = acc

    @pl.when(j == n_steps - 1)
    def _():
        o = acc_ref[...] / l_ref[...]
        lane_head = _head_of_lane((t, GROUP), 1)
        out = jnp.zeros((t, GROUP), F32)
        for h in range(N_HEADS):
            out = out + jnp.where(lane_head == h, o[h * t:(h + 1) * t, :], 0.0)
        oa_ref[0] = out


def _decode_att(sel, qr, kn, vn, cache_kt, cache_vt, page_table, layer, pages_per_step=8):
    db, t, _ = qr.shape
    n_steps = page_table.shape[1] // pages_per_step
    rows = N_HEADS * t

    def page_map(i):
        return lambda b, j, pt: (layer, pt[b, pages_per_step * j + i], 0, 0, 0)

    pages = [pl.BlockSpec((1, 1, N_HEADS, HEAD_DIM, PAGE), page_map(i)) for i in range(pages_per_step)]
    grid_spec = pltpu.PrefetchScalarGridSpec(
        num_scalar_prefetch=1,
        grid=(db, n_steps),
        in_specs=[pl.BlockSpec((1, rows, LANES), lambda b, j, pt: (b, 0, 0)),
                  pl.BlockSpec((1, t, GROUP), lambda b, j, pt: (b, 0, 0)),
                  pl.BlockSpec((1, t, GROUP), lambda b, j, pt: (b, 0, 0)),
                  pl.BlockSpec((1, t, GROUP), lambda b, j, pt: (b, 0, 0))] + pages + pages,
        out_specs=[pl.BlockSpec((1, t, GROUP), lambda b, j, pt: (b, 0, 0))],
        scratch_shapes=[pltpu.VMEM((rows, GROUP), BF16),
                        pltpu.VMEM((rows, GROUP), F32),
                        pltpu.VMEM((rows, 1), F32),
                        pltpu.VMEM((rows, 1), F32)],
    )
    return pl.pallas_call(
        functools.partial(_decode_att_kernel, pages_per_step=pages_per_step),
        grid_spec=grid_spec,
        out_shape=[jax.ShapeDtypeStruct((db, t, GROUP), F32)],
        compiler_params=_cparams(2),
        name="decode_att",
    )(page_table, sel, qr, kn, vn, *([cache_kt] * pages_per_step), *([cache_vt] * pages_per_step))[0]


def _conv_kernel(conf_ref, sc_ref, cbuf_ref, sbuf_ref, dw_ref, dwb_ref, lng_ref, lnb_ref, scw_ref,
                 ob_ref, od_ref, cnew_ref, snew_ref, extc_ref, exts_ref):
    ti = pl.program_id(1)
    nt = pl.num_programs(1)
    tt = conf_ref.shape[1]
    hc = 32
    hs = 8
    nbc = CONF_TAPS - 1
    nbs = SC_TAPS - 1

    @pl.when(ti == 0)
    def _():
        extc_ref[0:hc - nbc, :] = jnp.zeros((hc - nbc, GROUP), F32)
        extc_ref[hc - nbc:hc, :] = cbuf_ref[0]
        exts_ref[0:hs - nbs, :] = jnp.zeros((hs - nbs, GROUP), F32)
        exts_ref[hs - nbs:hs, :] = sbuf_ref[0]

    c = conf_ref[0]
    glu = c[:, 0:GROUP] * _sigmoid(c[:, GROUP:2 * GROUP])
    extc_ref[hc:hc + tt, :] = glu
    acc = jnp.zeros((tt, GROUP), F32) + dwb_ref[...]
    for j in range(CONF_TAPS):
        acc = acc + dw_ref[j:j + 1, :] * extc_ref[pl.ds(hc - nbc + j, tt), :]
    mu = jnp.mean(acc, axis=-1, keepdims=True)
    var = jnp.mean(jnp.square(acc - mu), axis=-1, keepdims=True)
    ln = (acc - mu) * lax.rsqrt(var + LN_EPS) * lng_ref[...] + lnb_ref[...]
    ob_ref[0] = ln * _sigmoid(ln)

    s = sc_ref[0]
    gated = s[:, GROUP:2 * GROUP] * s[:, 2 * GROUP:3 * GROUP]
    exts_ref[hs:hs + tt, :] = gated
    d = jnp.zeros((tt, GROUP), F32)
    for j in range(SC_TAPS):
        d = d + scw_ref[j:j + 1, :] * exts_ref[pl.ds(hs - nbs + j, tt), :]
    od_ref[0] = s[:, 0:GROUP] * d

    @pl.when(ti == nt - 1)
    def _():
        cnew_ref[0] = extc_ref[pl.ds(tt + hc - nbc, nbc), :]
        snew_ref[0] = exts_ref[pl.ds(tt + hs - nbs, nbs), :]

    tail_c = extc_ref[pl.ds(tt, hc), :]
    tail_s = exts_ref[pl.ds(tt, hs), :]
    extc_ref[0:hc, :] = tail_c
    exts_ref[0:hs, :] = tail_s


def _conv(conf, sc, cbuf, sbuf, dw, dwb, lng, lnb, scw, tt):
    b, t, _ = conf.shape
    nbc = CONF_TAPS - 1
    nbs = SC_TAPS - 1
    const = lambda i, j: (0, 0)
    return pl.pallas_call(
        _conv_kernel,
        grid=(b, t // tt),
        in_specs=[pl.BlockSpec((1, tt, 2 * GROUP), lambda i, j: (i, j, 0)),
                  pl.BlockSpec((1, tt, 3 * GROUP), lambda i, j: (i, j, 0)),
                  pl.BlockSpec((1, nbc, GROUP), lambda i, j: (i, 0, 0)),
                  pl.BlockSpec((1, nbs, GROUP), lambda i, j: (i, 0, 0)),
                  pl.BlockSpec((CONF_TAPS, GROUP), const),
                  pl.BlockSpec((1, GROUP), const),
                  pl.BlockSpec((1, GROUP), const),
                  pl.BlockSpec((1, GROUP), const),
                  pl.BlockSpec((SC_TAPS, GROUP), const)],
        out_specs=[pl.BlockSpec((1, tt, GROUP), lambda i, j: (i, j, 0)),
                   pl.BlockSpec((1, tt, GROUP), lambda i, j: (i, j, 0)),
                   pl.BlockSpec((1, nbc, GROUP), lambda i, j: (i, 0, 0)),
                   pl.BlockSpec((1, nbs, GROUP), lambda i, j: (i, 0, 0))],
        out_shape=[jax.ShapeDtypeStruct((b, t, GROUP), F32),
                   jax.ShapeDtypeStruct((b, t, GROUP), F32),
                   jax.ShapeDtypeStruct((b, nbc, GROUP), F32),
                   jax.ShapeDtypeStruct((b, nbs, GROUP), F32)],
        scratch_shapes=[pltpu.VMEM((32 + tt, GROUP), F32),
                        pltpu.VMEM((8 + tt, GROUP), F32)],
        compiler_params=_cparams(2),
        name="conv",
    )(conf, sc, cbuf, sbuf, dw, dwb, lng, lnb, scw)


def _rwkv_kernel(rw_ref, shift_ref, st0_ref, mu_ref, w0_ref, a0_ref, kkw_ref, ka_ref, rk_ref,
                 lng_ref, lnb_ref, w2_ref, a2_ref, g2_ref, oc_ref, sto_ref, st_ref, prev_ref):
    c = pl.program_id(1)

    @pl.when(c == 0)
    def _():
        st_ref[...] = st0_ref[...]
        prev_ref[...] = shift_ref[...]

    nseq, L, _ = rw_ref.shape
    SL = max(N_HEADS * L, LANES)
    seqs = range(nseq)
    each = lambda f, *cols: [f(*args) for args in zip(*cols)]
    P = [_rwkv_prep(s, rw_ref, mu_ref, w0_ref, a0_ref, kkw_ref, ka_ref, w2_ref, a2_ref, g2_ref, prev_ref)
         for s in seqs]
    at_s, bt_s, kt_s, rt_s, bg_s, kg_s, v_s = ([p[i] for p in P] for i in range(7))

    srow = lax.broadcasted_iota(jnp.int32, (SL, SL), 0)
    scol = lax.broadcasted_iota(jnp.int32, (SL, SL), 1)
    same = (srow // L) == (scol // L)
    strict = same & ((srow % L) > (scol % L))
    incl = same & ((srow % L) >= (scol % L))
    eye = jnp.where(srow == scol, 1.0, 0.0)
    n_ab = each(lambda a, b: jnp.where(strict, _mm_nt(a, b), 0.0), at_s, bt_s)
    n_ak = each(lambda a, b: jnp.where(strict, _mm_nt(a, b), 0.0), at_s, kt_s)
    q_rb = each(lambda a, b: jnp.where(incl, _mm_nt(a, b), 0.0), rt_s, bt_s)
    q_rk = each(lambda a, b: jnp.where(incl, _mm_nt(a, b), 0.0), rt_s, kt_s)

    tm = [eye + n for n in n_ab]
    pw = n_ab
    for _ in range(int(math.log2(L)) - 1):
        pw = each(_mm, pw, pw)
        tm = each(lambda t, x: t + _mm(t, x), tm, pw)

    p1 = each(_mm, tm, at_s)
    p2 = each(_mm, tm, each(_mm, n_ak, v_s))
    bg_t = [u.T for u in bg_s]
    kg_t = [u.T for u in kg_s]
    drow = lax.broadcasted_iota(jnp.int32, (GROUP, GROUP), 0)
    dcol = lax.broadcasted_iota(jnp.int32, (GROUP, GROUP), 1)
    m_mat = each(lambda p, bt, x: jnp.where(drow == dcol, jnp.exp(p[7]), 0.0) + _mm3(bt, x), P, bg_t, p1)
    c_mat = each(lambda bt, x, kt, vs: _mm3(bt, x) + _mm3(kt, vs), bg_t, p2, kg_t, v_s)
    yc = each(lambda rt, q, x: rt + _mm(q, x), rt_s, q_rb, p1)
    y0 = each(lambda q, x, qk, vs: _mm(q, x) + _mm(qk, vs), q_rb, p2, q_rk, v_s)

    st = [st_ref[s] for s in seqs]
    y_s = each(lambda a, b, y: _mm3(a, b) + y, yc, st, y0)
    st_new = each(lambda m, b, cm: _mm3(m, b) + cm, m_mat, st, c_mat)
    bd = _head_sum_matrix()
    inv_n = 1.0 / HEAD_DIM
    for s in seqs:
        st_ref[s] = st_new[s]
        y = y_s[s][0:L, :]
        for h in range(1, N_HEADS):
            y = y + y_s[s][h * L:(h + 1) * L, :]
        r, k2, v, g = P[s][8:12]
        mean = _mm2_exact_rhs(y, bd) * inv_n
        yc0 = y - mean
        var = _mm2_exact_rhs(yc0 * yc0, bd) * inv_n
        yn = yc0 * lax.rsqrt(var + RWKV_GN_EPS) * lng_ref[...] + lnb_ref[...]
        bonus = _mm2_exact_rhs(r * k2 * rk_ref[...], bd) * v
        oc_ref[s] = (yn + bonus) * g

    @pl.when(c == pl.num_programs(1) - 1)
    def _():
        sto_ref[...] = st_ref[...]


def _rwkv_prep(s, rw_ref, mu_ref, w0_ref, a0_ref, kkw_ref, ka_ref, w2_ref, a2_ref, g2_ref, prev_ref):
    L = rw_ref.shape[1]
    SL = max(N_HEADS * L, LANES)
    bd = _head_sum_matrix()

    z = rw_ref[s]
    row = lax.broadcasted_iota(jnp.int32, z.shape, 0)
    zprev = jnp.where(row == 0, prev_ref[s], pltpu.roll(z, 1, axis=0))
    prev_ref[s] = z[L - 1:L, :]
    zs = z + (zprev - z) * mu_ref[...]
    r = zs[:, 0:GROUP]
    k = zs[:, GROUP:2 * GROUP]
    v = zs[:, 2 * GROUP:3 * GROUP]
    low = zs[:, 3 * GROUP:3 * GROUP + RWKV_LOW]

    x = -(w0_ref[...] + _mm3(jnp.tanh(low), w2_ref[...]))
    softplus = jnp.maximum(x, 0.0) + jnp.log(1.0 + jnp.exp(-jnp.abs(x)))
    ld = -jnp.exp(-softplus - 0.5)
    a = _sigmoid(a0_ref[...] + _mm3(low, a2_ref[...]))
    g = _mm(_sigmoid(low), g2_ref[...])
    kk = k * kkw_ref[...]
    kk = kk / jnp.maximum(jnp.sqrt(_mm2_exact_rhs(kk * kk, bd)), 1e-12)
    k2 = k * (1.0 + (a - 1.0) * ka_ref[...])

    trow = lax.broadcasted_iota(jnp.int32, (L, L), 0)
    tcol = lax.broadcasted_iota(jnp.int32, (L, L), 1)
    tri = jnp.where(trow >= tcol, 1.0, 0.0).astype(BF16)
    cum = _mm2_exact_rhs_lhs(tri, ld)
    cend = cum[L - 1:L, :]
    g_prev = jnp.exp(cum - ld)
    g_inv = jnp.exp(-cum)
    g_rel = jnp.exp(cend - cum)
    kka = kk * a

    lane_head = _head_of_lane((L, GROUP), 1)

    def stack(u):
        parts = [jnp.where(lane_head == h, u, 0.0) for h in range(N_HEADS)]
        if SL > N_HEADS * L:
            parts.append(jnp.zeros((SL - N_HEADS * L, GROUP), F32))
        return jnp.concatenate(parts, axis=0)

    at_s = stack(-kk * g_prev)
    bt_s = stack(kka * g_inv)
    kt_s = stack(k2 * g_inv)
    rt_s = stack(r * jnp.exp(cum))
    bg_s = stack(kka * g_rel)
    kg_s = stack(k2 * g_rel)
    v_s = stack(v)
    return at_s, bt_s, kt_s, rt_s, bg_s, kg_s, v_s, cend, r, k2, v, g


def _mm2_exact_rhs_lhs(a_bf16, b):
    bh, bl = _split(b)
    d = functools.partial(jnp.dot, preferred_element_type=F32)
    return d(a_bf16, bh) + d(a_bf16, bl)


def _rwkv(rw, shift, st0, p, L, nseq=2):
    b, t, w = rw.shape
    const = lambda i, j: (0, 0)
    vec = pl.BlockSpec((1, GROUP), const)
    low = pl.BlockSpec((RWKV_LOW, GROUP), const)
    return pl.pallas_call(
        _rwkv_kernel,
        grid=(b // nseq, t // L),
        in_specs=[pl.BlockSpec((nseq, L, w), lambda i, j: (i, j, 0)),
                  pl.BlockSpec((nseq, 1, w), lambda i, j: (i, 0, 0)),
                  pl.BlockSpec((nseq, GROUP, GROUP), lambda i, j: (i, 0, 0)),
                  pl.BlockSpec((1, w), const),
                  vec, vec, vec, vec, vec, vec, vec, low, low, low],
        out_specs=[pl.BlockSpec((nseq, L, GROUP), lambda i, j: (i, j, 0)),
                   pl.BlockSpec((nseq, GROUP, GROUP), lambda i, j: (i, 0, 0))],
        out_shape=[jax.ShapeDtypeStruct((b, t, GROUP), F32),
                   jax.ShapeDtypeStruct((b, GROUP, GROUP), F32)],
        scratch_shapes=[pltpu.VMEM((nseq, GROUP, GROUP), F32),
                        pltpu.VMEM((nseq, 1, w), F32)],
        compiler_params=_cparams(2),
        name="rwkv",
    )(rw, shift, st0, p["mu"], p["w0"], p["a0"], p["kkw"], p["ka"], p["rk"], p["lng"], p["lnb"],
      p["w2"], p["a2"], p["g2"])


def _state_to_blockdiag(s):
    b = s.shape[0]
    st = jnp.swapaxes(s, -1, -2)
    eye = jnp.eye(N_HEADS, dtype=s.dtype)
    return jnp.einsum("bhkv,hg->bhkgv", st, eye).reshape(b, GROUP, GROUP)


def _blockdiag_to_state(st):
    b = st.shape[0]
    x = st.reshape(b, N_HEADS, HEAD_DIM, N_HEADS, HEAD_DIM)
    diag = jnp.stack([x[:, h, :, h, :] for h in range(N_HEADS)], axis=1)
    return jnp.swapaxes(diag, -1, -2)


def _outproj_kernel(oa_ref, ob_ref, oc_ref, od_ref, x_ref, wo_ref, g_ref, wq_ref, x1_ref, h_ref, q_ref):
    acc = x_ref[...]
    for i, o_ref in enumerate((oa_ref, ob_ref, oc_ref, od_ref)):
        acc = acc + jnp.dot(o_ref[...].astype(BF16), wo_ref[i], preferred_element_type=F32)
    x1_ref[...] = acc
    ms = jnp.mean(acc * acc, axis=-1, keepdims=True)
    h = (acc * lax.rsqrt(ms + RMS_EPS) * g_ref[...]).astype(BF16)
    h_ref[...] = h
    q = jnp.dot(h, wq_ref[...], preferred_element_type=F32)
    qd = q_ref.shape[2]
    for hd in range(q_ref.shape[0]):
        q_ref[hd] = q[:, hd * qd:(hd + 1) * qd]


def _outproj(oa, ob, oc, od, x, wo, g, wq, tm):
    n, d = x.shape
    dq = wq.shape[1]
    mix = pl.BlockSpec((tm, GROUP), lambda i: (i, 0))
    return pl.pallas_call(
        _outproj_kernel,
        grid=(n // tm,),
        in_specs=[mix, mix, mix, mix,
                  pl.BlockSpec((tm, d), lambda i: (i, 0)),
                  pl.BlockSpec((4, GROUP, d), lambda i: (0, 0, 0)),
                  pl.BlockSpec((1, d), lambda i: (0, 0)),
                  pl.BlockSpec((d, dq), lambda i: (0, 0))],
        out_specs=[pl.BlockSpec((tm, d), lambda i: (i, 0)),
                   pl.BlockSpec((tm, d), lambda i: (i, 0)),
                   pl.BlockSpec((PEER_HEADS, tm, dq // PEER_HEADS), lambda i: (0, i, 0))],
        out_shape=[jax.ShapeDtypeStruct((n, d), F32),
                   jax.ShapeDtypeStruct((n, d), BF16),
                   jax.ShapeDtypeStruct((PEER_HEADS, n, dq // PEER_HEADS), F32)],
        compiler_params=_cparams(1),
        name="outproj",
    )(oa, ob, oc, od, x, wo, g, wq)


_CAND_PIECES = ((0, 0, 8), (0, 8, 8), (1, 0, 8), (2, 0, 5), (3, 0, 4), (4, 0, 3), (5, 0, 2),
                (6, 0, 2), (7, 0, 2))
_CAND_ROWS = 8 * (len(_CAND_PIECES) + 1)


def _sort_network(n):
    size = 1
    while size < n:
        size *= 2
    pairs = []
    p = 1
    while p < size:
        k = p
        while k >= 1:
            for j in range(k % p, size - k, 2 * k):
                for i in range(min(k, size - j - k)):
                    if (i + j) // (2 * p) == (i + j + k) // (2 * p):
                        pairs.append((i + j, i + j + k))
            k //= 2
        p *= 2
    return tuple((i, j) for i, j in pairs if j < n)


_SUBLANES = 8


def _top_values(x, depth):
    groups = x.shape[0] // _SUBLANES
    v = [x[_SUBLANES * k:_SUBLANES * (k + 1), :] for k in range(groups)]
    for i, j in _sort_network(groups):
        v[i], v[j] = jnp.maximum(v[i], v[j]), jnp.minimum(v[i], v[j])
    sidx = lax.broadcasted_iota(jnp.int32, v[0].shape, 0).astype(F32)
    out = []
    for r in range(depth):
        mx = jnp.max(v[0], axis=0, keepdims=True)
        first = jnp.min(jnp.where(v[0] == mx, sidx, float(_SUBLANES)), axis=0, keepdims=True)
        pop = sidx == first
        out.append(mx)
        reachable = depth - 1 - r
        for k in range(min(groups - 1, reachable)):
            v[k] = jnp.where(pop, v[k + 1], v[k])
        if reachable > groups - 1:
            v[groups - 1] = jnp.where(pop, -jnp.inf, v[groups - 1])
    return jnp.concatenate(out, axis=0)


_SELECT_TILES_PER_ITER = 2


def _peer_select_kernel(q_ref, k1_ref, k2_ref, s1_ref, s2_ref, thr_ref, stat_ref):
    tn = q_ref.shape[1]
    n_lt = tn // LANES
    q = q_ref[0].astype(BF16)
    nt = functools.partial(lax.dot_general, dimension_numbers=(((1,), (1,)), ((), ())),
                           preferred_element_type=F32)
    s1_all = nt(k1_ref[...], q)
    s2_all = nt(k2_ref[...], q)
    for lt in range(n_lt):
        s1_ref[lt, 0] = s1_all[:, lt * LANES:(lt + 1) * LANES]
        s2_ref[lt, 0] = s2_all[:, lt * LANES:(lt + 1) * LANES]

    sub = lax.broadcasted_iota(jnp.int32, (8, LANES), 0)

    def one_tile(lt):
        s1 = s1_ref[lt, 0]
        v1 = _top_values(s1, PEER_TOPK)
        v2 = _top_values(s2_ref[lt, 0], PEER_TOPK)
        pieces = [jnp.where(sub < nb, v1[a:a + 1, :] + v2[b0:b0 + 8, :], -jnp.inf)
                  for a, b0, nb in _CAND_PIECES]
        pieces.append(v1[8:16, :] + v2[0:1, :])
        cv = _top_values(jnp.concatenate(pieces, axis=0), PEER_TOPK)
        tau = cv[PEER_TOPK - 1:PEER_TOPK, :]
        z = jnp.sum(jnp.exp(cv - cv[0:1, :]), axis=0, keepdims=True)
        stat_ref[lt, 0] = jnp.concatenate(
            [tau, v1[0:1, :], v2[0:1, :], 1.0 / z, jnp.zeros((4, LANES), F32)], axis=0)
        thr = jnp.full((PEER_NKEYS, LANES), jnp.inf, F32)
        for b in range(PEER_TOPK):
            vb = v2[b:b + 1, :]
            thr = jnp.where((s1 + vb) >= tau, vb, thr)
        thr_ref[lt, 0] = thr

    per_iter = math.gcd(n_lt, _SELECT_TILES_PER_ITER)

    def tile_body(i, carry):
        for u in range(per_iter):
            one_tile(i * per_iter + u)
        return carry

    lax.fori_loop(0, n_lt // per_iter, tile_body, 0)


def _peer_select(q, k1p, k2p, tn):
    _, n, qd = q.shape
    n_lt = tn // LANES
    keyed = pl.BlockSpec((n_lt, 1, PEER_NKEYS, LANES), lambda i, h: (i, h, 0, 0))
    keyed_shape = jax.ShapeDtypeStruct((n // LANES, PEER_HEADS, PEER_NKEYS, LANES), F32)
    return pl.pallas_call(
        _peer_select_kernel,
        grid=(n // tn, PEER_HEADS),
        in_specs=[pl.BlockSpec((1, tn, qd), lambda i, h: (h, i, 0)),
                  pl.BlockSpec((PEER_NKEYS, qd), lambda i, h: (0, 0)),
                  pl.BlockSpec((PEER_NKEYS, qd), lambda i, h: (0, 0))],
        out_specs=[keyed, keyed, keyed,
                   pl.BlockSpec((n_lt, 1, 8, LANES), lambda i, h: (i, h, 0, 0))],
        out_shape=[keyed_shape, keyed_shape, keyed_shape,
                   jax.ShapeDtypeStruct((n // LANES, PEER_HEADS, 8, LANES), F32)],
        compiler_params=_cparams(2),
        name="peer_select",
    )(q, k1p, k2p)


_PEER_SUB_KEYS = 2


def _peer_dense_kernel(h_ref, u_ref, vt_ref, s1_ref, s2_ref, thr_ref, stat_ref, x1_ref, y_ref,
                       acc_ref, e1_ref, e2_ref):
    j = pl.program_id(1)
    nj = pl.num_programs(1)
    eb = u_ref.shape[0]
    n_lt = s2_ref.shape[0]
    ne1 = eb // PEER_NKEYS

    @pl.when(j == 0)
    def _():
        acc_ref[...] = jnp.zeros(acc_ref.shape, F32)
        for lt in range(n_lt):
            for h in range(PEER_HEADS):
                st = stat_ref[lt, h]
                e1_ref[lt, h] = jnp.exp(s1_ref[lt, h] - st[1:2, :]) * st[3:4, :]
                e2_ref[lt, h] = jnp.exp(s2_ref[lt, h] - st[2:3, :])

    hx = h_ref[...]
    sub_rows = _PEER_SUB_KEYS * PEER_NKEYS
    dep = jnp.zeros((1, LANES), F32)
    n_sub = ne1 // _PEER_SUB_KEYS

    def activations(sb):
        return lax.dot_general(u_ref[sb * sub_rows:(sb + 1) * sub_rows, :], hx, (((1,), (1,)), ((), ())),
                               preferred_element_type=F32)

    act_next = activations(0)
    for sb in range(n_sub):
        rows = slice(sb * sub_rows, (sb + 1) * sub_rows)
        act = act_next
        if sb + 1 < n_sub:
            act_next = activations(sb + 1)
        w_rows = []
        for e in range(_PEER_SUB_KEYS):
            e1 = j * ne1 + sb * _PEER_SUB_KEYS + e
            w_tiles = []
            for lt in range(n_lt):
                gate = jnp.zeros((PEER_NKEYS, LANES), F32)
                for h in range(PEER_HEADS):
                    chosen = s2_ref[lt, h] >= (thr_ref[lt, h, pl.ds(e1, 1), :] + dep)
                    gate = gate + jnp.where(chosen, e2_ref[lt, h], 0.0) * e1_ref[lt, h, pl.ds(e1, 1), :]
                a = act[e * PEER_NKEYS:(e + 1) * PEER_NKEYS, lt * LANES:(lt + 1) * LANES]
                wt = gate * (0.5 * a * (1.0 + lax.erf(a * (2.0 ** -0.5))))
                bits = pltpu.bitcast(wt[0:1, :], jnp.uint32)
                dep = pltpu.bitcast(lax.shift_right_logical(lax.shift_right_logical(bits, jnp.uint32(16)),
                                                            jnp.uint32(16)), F32)
                w_tiles.append(wt.astype(BF16))
            w_rows.append(jnp.concatenate(w_tiles, axis=1))
        w = jnp.concatenate(w_rows, axis=0)
        acc_ref[...] += jnp.dot(vt_ref[0, :, rows], w, preferred_element_type=F32)

    @pl.when(j == nj - 1)
    def _():
        y_ref[...] = x1_ref[...] + acc_ref[...].T


def _peer_dense(h, u, vt, s1, s2, thr, stat, x1, tn, eb):
    n, d = x1.shape
    ne = u.shape[0]
    n_lt = tn // LANES
    keyed = pl.BlockSpec((n_lt, PEER_HEADS, PEER_NKEYS, LANES), lambda i, j: (i, 0, 0, 0))
    return pl.pallas_call(
        _peer_dense_kernel,
        grid=(n // tn, ne // eb),
        in_specs=[pl.BlockSpec((tn, d), lambda i, j: (i, 0)),
                  pl.BlockSpec((eb, d), lambda i, j: (j, 0)),
                  pl.BlockSpec((1, d, eb), lambda i, j: (j, 0, 0)),
                  keyed, keyed, keyed,
                  pl.BlockSpec((n_lt, PEER_HEADS, 8, LANES), lambda i, j: (i, 0, 0, 0)),
                  pl.BlockSpec((tn, d), lambda i, j: (i, 0))],
        out_specs=[pl.BlockSpec((tn, d), lambda i, j: (i, 0))],
        out_shape=[jax.ShapeDtypeStruct((n, d), F32)],
        scratch_shapes=[pltpu.VMEM((d, tn), F32),
                        pltpu.VMEM((n_lt, PEER_HEADS, PEER_NKEYS, LANES), F32),
                        pltpu.VMEM((n_lt, PEER_HEADS, PEER_NKEYS, LANES), F32)],
        compiler_params=_cparams(2),
        name="peer_dense",
    )(h, u, vt, s1, s2, thr, stat, x1)[0]


def _rope_tables(pos):
    half = HEAD_DIM // 2
    inv_freq = ROPE_THETA ** (-jnp.arange(half, dtype=F32) / half)
    ang = pos.astype(F32)[:, None] * inv_freq[None, :]
    cos = jnp.tile(jnp.cos(ang), (1, 2 * N_HEADS))
    sin = jnp.sin(ang)
    sin_signed = jnp.tile(jnp.concatenate([-sin, sin], axis=1), (1, N_HEADS))
    return cos, sin_signed


def _row_tile(n, pref):
    return pref if n % pref == 0 else n


def _seq_group(b, pref):
    return math.gcd(b, pref)


def _layer_params(l, norm_mix, w_in, q_norm, k_norm, conf_dw, conf_dw_b, conf_ln_g, conf_ln_b,
                  rwkv_mu, rwkv_w0, rwkv_w2, rwkv_a0, rwkv_a2, rwkv_g2, rwkv_kk, rwkv_ka, rwkv_rk,
                  rwkv_ln_g, rwkv_ln_b, sc_w, w_out, norm_ffn, peer_wq, peer_k1, peer_k2, peer_u, peer_v):
    row = lambda a: a[l].reshape(1, -1)
    att_c = 3 * GROUP
    conf_c = 2 * GROUP
    rw_c = rwkv_mu.shape[1]
    wi = w_in[l].astype(BF16)
    bounds = [0, att_c, att_c + conf_c, att_c + conf_c + rw_c, wi.shape[1]]
    rd, ra, rg = rwkv_w2.shape[1], rwkv_a2.shape[1], rwkv_g2.shape[1]
    zeros = lambda r: jnp.zeros((r, GROUP), F32)
    half = peer_k1.shape[2]
    return dict(
        norm_mix=row(norm_mix),
        w_in=[wi[:, bounds[i]:bounds[i + 1]] for i in range(4)],
        qg=jnp.tile(row(q_norm), (1, N_HEADS)), kg=jnp.tile(row(k_norm), (1, N_HEADS)),
        dw=conf_dw[l], dwb=row(conf_dw_b), clng=row(conf_ln_g), clnb=row(conf_ln_b), scw=sc_w[l],
        rwkv=dict(mu=row(rwkv_mu), w0=row(rwkv_w0), a0=row(rwkv_a0), kkw=row(rwkv_kk), ka=row(rwkv_ka),
                  rk=row(rwkv_rk), lng=row(rwkv_ln_g), lnb=row(rwkv_ln_b),
                  w2=jnp.concatenate([rwkv_w2[l], zeros(ra + rg)], axis=0),
                  a2=jnp.concatenate([zeros(rd), rwkv_a2[l], zeros(rg)], axis=0),
                  g2=jnp.concatenate([zeros(rd + ra), rwkv_g2[l]], axis=0)),
        w_out=w_out[l].astype(BF16).reshape(4, GROUP, -1),
        norm_ffn=row(norm_ffn), wq=peer_wq[l].astype(BF16),
        k1p=jnp.concatenate([peer_k1[l], jnp.zeros_like(peer_k1[l])], axis=1).astype(BF16),
        k2p=jnp.concatenate([jnp.zeros_like(peer_k2[l]), peer_k2[l]], axis=1).astype(BF16),
        u=peer_u[l].astype(BF16),
        vt=jnp.swapaxes(peer_v[l].astype(BF16).reshape(-1, PEER_EXPERT_BLOCK, peer_v.shape[2]), 1, 2),
    )


def _mix_and_ffn(x, oa, ob, oc, od, p):
    b, t, d = x.shape
    n = b * t
    flat = lambda a: a.reshape(n, a.shape[-1])
    tm = _row_tile(n, 512)
    x1, h2, q = _outproj(flat(oa), flat(ob), flat(oc), flat(od), flat(x), p["w_out"], p["norm_ffn"],
                         p["wq"], tm)
    tn = _row_tile(n, 512)
    s1, s2, thr, stat = _peer_select(q, p["k1p"], p["k2p"], tn)
    y = _peer_dense(h2, p["u"], p["vt"], s1, s2, thr, stat, x1, tn, PEER_EXPERT_BLOCK)
    return y.reshape(b, t, d)


def _common_mixers(x, p, conf_buf, shift_prev, wkv_prev, sc_buf, chunk, nseq, conv_tile):
    b, t, d = x.shape
    n = b * t
    att, conf, rw, sc = _proj(x.reshape(n, d), p["norm_mix"], p["w_in"], _row_tile(n, 512))
    att = att.reshape(b, t, -1)
    conf = conf.reshape(b, t, -1)
    rw = rw.reshape(b, t, -1)
    sc = sc.reshape(b, t, -1)
    ob, od, conf_new, sc_new = _conv(conf, sc, conf_buf, sc_buf, p["dw"], p["dwb"], p["clng"], p["clnb"],
                                     p["scw"], conv_tile)
    oc, st_out = _rwkv(rw, shift_prev[:, None, :], _state_to_blockdiag(wkv_prev), p["rwkv"], chunk, nseq)
    return att, ob, oc, od, conf_new, rw[:, -1, :], _blockdiag_to_state(st_out), sc_new


def kernel(x_prompt, x_sample, cache_k, cache_v, page_table, state_conf_conv, state_rwkv_shift, state_rwkv_wkv, state_short_conv, norm_mix, w_in, q_norm, k_norm, conf_dw, conf_dw_b, conf_ln_g, conf_ln_b, rwkv_mu, rwkv_w0, rwkv_w2, rwkv_a0, rwkv_a2, rwkv_g2, rwkv_kk, rwkv_ka, rwkv_rk, rwkv_ln_g, rwkv_ln_b, sc_w, w_out, norm_ffn, peer_wq, peer_k1, peer_k2, peer_u, peer_v):
    depth = w_in.shape[0]
    bp, seq, _ = x_prompt.shape
    db, dec_seq, _ = x_sample.shape
    n_pool = cache_k.shape[1]
    past_len = page_table.shape[1] * PAGE
    ck = jnp.transpose(cache_k, (0, 1, 3, 4, 2))
    cv = jnp.transpose(cache_v, (0, 1, 3, 4, 2))
    cos_p, sin_p = _rope_tables(jnp.arange(seq))
    cos_s, sin_s = _rope_tables(past_len + jnp.arange(dec_seq))
    dt = x_prompt.dtype

    yp, ys = x_prompt, x_sample
    outs_p = [[] for _ in range(6)]
    outs_s = [[] for _ in range(6)]
    for l in range(depth):
        p = _layer_params(l, norm_mix, w_in, q_norm, k_norm, conf_dw, conf_dw_b, conf_ln_g, conf_ln_b,
                          rwkv_mu, rwkv_w0, rwkv_w2, rwkv_a0, rwkv_a2, rwkv_g2, rwkv_kk, rwkv_ka,
                          rwkv_rk, rwkv_ln_g, rwkv_ln_b, sc_w, w_out, norm_ffn, peer_wq, peer_k1,
                          peer_k2, peer_u, peer_v)
        att, ob, oc, od, conf_new, shift_new, wkv_new, sc_new = _common_mixers(
            yp, p,
            jnp.zeros((bp, CONF_TAPS - 1, GROUP), dt),
            jnp.zeros((bp, rwkv_mu.shape[1]), dt),
            jnp.zeros((bp, N_HEADS, HEAD_DIM, HEAD_DIM), dt),
            jnp.zeros((bp, SC_TAPS - 1, GROUP), dt),
            chunk=64, nseq=_seq_group(bp, 4), conv_tile=_row_tile(seq, 512))
        oa, k_rot = _moba_prompt(att, cos_p, sin_p, p["qg"], p["kg"])
        new_p = (k_rot.reshape(bp, seq, N_HEADS, HEAD_DIM),
                 att[:, :, 2 * GROUP:].reshape(bp, seq, N_HEADS, HEAD_DIM),
                 conf_new, shift_new, wkv_new, sc_new)
        yp = _mix_and_ffn(yp, oa, ob, oc, od, p)

        att, ob, oc, od, conf_new, shift_new, wkv_new, sc_new = _common_mixers(
            ys, p, state_conf_conv[l], state_rwkv_shift[l], state_rwkv_wkv[l], state_short_conv[l],
            chunk=dec_seq, nseq=_seq_group(db, 8), conv_tile=dec_seq)
        sel, q_rot, k_rot = _decode_sel(att, cos_s, sin_s, p["qg"], p["kg"], ck, page_table, l)
        v_new = att[:, :, 2 * GROUP:]
        oa = _decode_att(sel, q_rot, k_rot, v_new, ck, cv, page_table, l)
        new_s = (k_rot.reshape(db, dec_seq, N_HEADS, HEAD_DIM),
                 v_new.reshape(db, dec_seq, N_HEADS, HEAD_DIM),
                 conf_new, shift_new, wkv_new, sc_new)
        ys = _mix_and_ffn(ys, oa, ob, oc, od, p)

        for lst, a in zip(outs_p, new_p):
            lst.append(a)
        for lst, a in zip(outs_s, new_s):
            lst.append(a)
    pk, pv, pconf, pshift, pwkv, psc = [jnp.stack(a) for a in outs_p]
    sk, sv, sconf, sshift, swkv, ssc = [jnp.stack(a) for a in outs_s]
    return (yp, ys, pk, pv, sk, sv, pconf, sconf, pshift, sshift, pwkv, swkv, psc, ssc)
```

```python
import functools
import math

import jax
import jax.numpy as jnp
from jax import lax
from jax.experimental import pallas as pl
from jax.experimental.pallas import tpu as pltpu

F32 = jnp.float32
BF16 = jnp.bfloat16

HEAD_DIM = 64
GROUP = 256
N_HEADS = GROUP // HEAD_DIM
MOBA_BLOCK = 256
MOBA_TOPK = 3
ROPE_THETA = 10000.0
PAGE = 128
CONF_TAPS = 31
SC_TAPS = 3
RWKV_LOW = 128
PEER_HEADS = 8
PEER_NKEYS = 128
PEER_TOPK = 16
PEER_EXPERT_BLOCK = 1024
RMS_EPS = 1e-6
LN_EPS = 1e-5
RWKV_GN_EPS = 64e-5
NEG_INF = -1e30
LANES = 128
VMEM_LIMIT = 56 * 1024 * 1024

_ARB = "arbitrary"


def _cparams(n_axes, vmem=VMEM_LIMIT):
    return pltpu.CompilerParams(dimension_semantics=(_ARB,) * n_axes, vmem_limit_bytes=vmem)


def _mm(a, b):
    return jnp.dot(a.astype(BF16), b.astype(BF16), preferred_element_type=F32)


def _mm_nt(a, b):
    return lax.dot_general(a.astype(BF16), b.astype(BF16), (((1,), (1,)), ((), ())),
                           preferred_element_type=F32)


def _split(a):
    hi = a.astype(BF16)
    lo = (a - hi.astype(F32)).astype(BF16)
    return hi, lo


def _mm3(a, b):
    ah, al = _split(a)
    bh, bl = _split(b)
    d = functools.partial(jnp.dot, preferred_element_type=F32)
    return d(ah, bh) + d(ah, bl) + d(al, bh)


def _mm3_nt(a, b):
    ah, al = _split(a)
    bh, bl = _split(b)
    d = functools.partial(lax.dot_general, dimension_numbers=(((1,), (1,)), ((), ())),
                          preferred_element_type=F32)
    return d(ah, bh) + d(ah, bl) + d(al, bh)


def _mm2_exact_rhs(a, b_bf16):
    ah, al = _split(a)
    d = functools.partial(jnp.dot, preferred_element_type=F32)
    return d(ah, b_bf16) + d(al, b_bf16)


def _sigmoid(x):
    return 1.0 / (1.0 + jnp.exp(-x))


def _head_of_lane(shape, axis):
    return lax.broadcasted_iota(jnp.int32, shape, axis) // HEAD_DIM


def _head_sum_matrix():
    r = _head_of_lane((GROUP, GROUP), 0)
    c = _head_of_lane((GROUP, GROUP), 1)
    return jnp.where(r == c, 1.0, 0.0).astype(BF16)


def _proj_kernel(x_ref, g_ref, wa_ref, wc_ref, wr_ref, ws_ref, oa_ref, oc_ref, or_ref, os_ref):
    x = x_ref[...]
    ms = jnp.mean(x * x, axis=-1, keepdims=True)
    h = (x * lax.rsqrt(ms + RMS_EPS) * g_ref[...]).astype(BF16)
    for w_ref, o_ref in ((wa_ref, oa_ref), (wc_ref, oc_ref), (wr_ref, or_ref), (ws_ref, os_ref)):
        o_ref[...] = jnp.dot(h, w_ref[...], preferred_element_type=F32)


def _proj(x, g, ws, tm):
    n, d = x.shape
    widths = [w.shape[1] for w in ws]
    return pl.pallas_call(
        _proj_kernel,
        grid=(n // tm,),
        in_specs=[pl.BlockSpec((tm, d), lambda i: (i, 0)),
                  pl.BlockSpec((1, d), lambda i: (0, 0))]
                 + [pl.BlockSpec((d, wd), lambda i: (0, 0)) for wd in widths],
        out_specs=[pl.BlockSpec((tm, wd), lambda i: (i, 0)) for wd in widths],
        out_shape=[jax.ShapeDtypeStruct((n, wd), F32) for wd in widths],
        compiler_params=_cparams(1),
        name="proj",
    )(x, g, *ws)


def _head_rms(x, gain, bd):
    ms = _mm2_exact_rhs(x * x, bd) * (1.0 / HEAD_DIM)
    return x * lax.rsqrt(ms + RMS_EPS) * gain


def _rope(x, cos, sin_signed):
    lane = lax.broadcasted_iota(jnp.int32, x.shape, 1) % HEAD_DIM
    lo = pltpu.roll(x, GROUP - HEAD_DIM // 2, axis=1)
    hi = pltpu.roll(x, HEAD_DIM // 2, axis=1)
    rot = jnp.where(lane < HEAD_DIM // 2, lo, hi)
    return x * cos + rot * sin_signed


def _top_rows(g, n_rows, k):
    ridx = lax.broadcasted_iota(jnp.int32, g.shape, 0).astype(F32)
    sel = jnp.zeros(g.shape, F32)
    cur = g
    for _ in range(k):
        mx = jnp.max(cur, axis=0, keepdims=True)
        first = jnp.min(jnp.where(cur == mx, ridx, float(n_rows)), axis=0, keepdims=True)
        hit = ridx == first
        sel = jnp.where(hit, 1.0, sel)
        cur = jnp.where(hit, -jnp.inf, cur)
    return sel


def _moba_prompt_kernel(att_ref, cos_ref, sin_ref, qg_ref, kg_ref, oa_ref, kr_ref,
                        kb_ref, vt_ref, km_ref, sel_ref, qm_ref):
    qi = pl.program_id(1)
    nblk = kb_ref.shape[0]
    blk = MOBA_BLOCK
    bd = _head_sum_matrix()

    @pl.when(qi == 0)
    def _():
        km_ref[...] = jnp.zeros(km_ref.shape, F32)

    a = att_ref[0]
    cos = cos_ref[...]
    sin = sin_ref[...]
    qr = _rope(_head_rms(a[:, 0:GROUP], qg_ref[...], bd), cos, sin)
    kr = _rope(_head_rms(a[:, GROUP:2 * GROUP], kg_ref[...], bd), cos, sin)
    v = a[:, 2 * GROUP:3 * GROUP]
    kr_ref[0] = kr
    kb_ref[qi] = kr.astype(BF16)
    vt_ref[qi] = v.T.astype(BF16)
    km_ref[pl.ds(qi, 1), :] = jnp.mean(kr, axis=0, keepdims=True)

    qs = qr * (HEAD_DIM ** -0.5)
    lane_head = _head_of_lane((blk, GROUP), 1)
    km = km_ref[...]
    km_head = _head_of_lane(km.shape, 1)
    blk_row = lax.broadcasted_iota(jnp.int32, (nblk, blk), 0)
    krow = lax.broadcasted_iota(jnp.int32, (blk, blk), 0)
    qcol = lax.broadcasted_iota(jnp.int32, (blk, blk), 1)
    kcur = kb_ref[qi]
    heads = range(N_HEADS)
    head_rows = [slice(h * HEAD_DIM, (h + 1) * HEAD_DIM) for h in heads]
    nt = functools.partial(lax.dot_general, dimension_numbers=(((1,), (1,)), ((), ())),
                           preferred_element_type=F32)
    qms = [jnp.where(lane_head == h, qs, 0.0) for h in heads]
    for h in heads:
        qm_ref[h] = qms[h].astype(BF16)
    past = blk_row < qi
    gates = [_mm3_nt(jnp.where(km_head == h, km, 0.0), qms[h]) for h in heads]
    for h in heads:
        sel = _top_rows(jnp.where(past, gates[h], NEG_INF), nblk, MOBA_TOPK)
        sel_ref[h] = jnp.where(past, sel, 0.0)
    sts = [jnp.where(krow <= qcol, nt(kcur, qm_ref[h]), NEG_INF) for h in heads]
    ms = [jnp.max(st, axis=0, keepdims=True) for st in sts]
    ps = [jnp.exp(st - m) for st, m in zip(sts, ms)]
    ls = [jnp.sum(p, axis=0, keepdims=True) for p in ps]
    accs = [jnp.dot(vt_ref[qi, head_rows[h], :], ps[h].astype(BF16), preferred_element_type=F32)
            for h in heads]

    def past_block(j, carry):
        ms, ls, accs = carry
        kj = kb_ref[j]
        sts = [jnp.where(sel_ref[h, pl.ds(j, 1), :] > 0.0, nt(kj, qm_ref[h]), NEG_INF) for h in heads]
        m_new = [jnp.maximum(m, jnp.max(st, axis=0, keepdims=True)) for m, st in zip(ms, sts)]
        alpha = [jnp.exp(m - mn) for m, mn in zip(ms, m_new)]
        ps = [jnp.exp(st - mn) for st, mn in zip(sts, m_new)]
        ls = [a * l + jnp.sum(p, axis=0, keepdims=True) for a, l, p in zip(alpha, ls, ps)]
        accs = [alpha[h] * accs[h] + jnp.dot(vt_ref[j, head_rows[h], :], ps[h].astype(BF16),
                                             preferred_element_type=F32) for h in heads]
        return m_new, ls, accs

    ms, ls, accs = lax.fori_loop(0, qi, past_block, (ms, ls, accs))
    oa_ref[0] = jnp.concatenate([acc / l for acc, l in zip(accs, ls)], axis=0).T


def _moba_prompt(att, cos, sin, qg, kg):
    b, t, _ = att.shape
    nblk = t // MOBA_BLOCK
    blk = MOBA_BLOCK
    return pl.pallas_call(
        _moba_prompt_kernel,
        grid=(b, nblk),
        in_specs=[pl.BlockSpec((1, blk, 3 * GROUP), lambda i, j: (i, j, 0)),
                  pl.BlockSpec((blk, GROUP), lambda i, j: (j, 0)),
                  pl.BlockSpec((blk, GROUP), lambda i, j: (j, 0)),
                  pl.BlockSpec((1, GROUP), lambda i, j: (0, 0)),
                  pl.BlockSpec((1, GROUP), lambda i, j: (0, 0))],
        out_specs=[pl.BlockSpec((1, blk, GROUP), lambda i, j: (i, j, 0)),
                   pl.BlockSpec((1, blk, GROUP), lambda i, j: (i, j, 0))],
        out_shape=[jax.ShapeDtypeStruct((b, t, GROUP), F32),
                   jax.ShapeDtypeStruct((b, t, GROUP), F32)],
        scratch_shapes=[pltpu.VMEM((nblk, blk, GROUP), BF16),
                        pltpu.VMEM((nblk, GROUP, blk), BF16),
                        pltpu.VMEM((nblk, GROUP), F32),
                        pltpu.VMEM((N_HEADS, nblk, blk), F32),
                        pltpu.VMEM((N_HEADS, blk, GROUP), BF16)],
        compiler_params=_cparams(2),
        name="moba_prompt",
    )(att, cos, sin, qg, kg)


def _stack_heads(x):
    lane_head = _head_of_lane(x.shape, 1)
    return jnp.concatenate([jnp.where(lane_head == h, x, 0.0) for h in range(N_HEADS)], axis=0)


def _decode_sel_kernel(pt_ref, att_ref, cos_ref, sin_ref, qg_ref, kg_ref, *rest, pages_per_step):
    page_refs = rest[:pages_per_step]
    sel_ref, qr_ref, kr_ref, q2_ref, gate_ref = rest[pages_per_step:]
    s = pl.program_id(1)
    n_steps = pl.num_programs(1)
    blocks_per_step = pages_per_step * PAGE // MOBA_BLOCK
    n_past = n_steps * blocks_per_step
    pages_per_block = MOBA_BLOCK // PAGE
    rows = gate_ref.shape[0]
    bd = _head_sum_matrix()

    @pl.when(s == 0)
    def _():
        gate_ref[...] = jnp.zeros(gate_ref.shape, F32)
        a = att_ref[0]
        qr = _rope(_head_rms(a[:, 0:GROUP], qg_ref[...], bd), cos_ref[...], sin_ref[...])
        qr_ref[0] = qr
        kr_ref[0] = _rope(_head_rms(a[:, GROUP:2 * GROUP], kg_ref[...], bd), cos_ref[...], sin_ref[...])
        hi, lo = _split(_stack_heads(qr * (HEAD_DIM ** -0.5)))
        q2_ref[...] = jnp.concatenate([hi, lo], axis=0)

    lane = lax.broadcasted_iota(jnp.int32, gate_ref.shape, 1)
    for bi in range(blocks_per_step):
        tot = jnp.zeros((rows, 1), F32)
        for pi in range(pages_per_block):
            kt = page_refs[bi * pages_per_block + pi][0, 0].reshape(GROUP, PAGE).astype(BF16)
            sc = jnp.dot(q2_ref[...], kt, preferred_element_type=F32)
            tot = tot + jnp.sum(sc[0:rows, :] + sc[rows:2 * rows, :], axis=1, keepdims=True)
        gate_ref[...] += jnp.where(lane == s * blocks_per_step + bi, tot * (1.0 / MOBA_BLOCK), 0.0)

    @pl.when(s == n_steps - 1)
    def _():
        gate = gate_ref[...]
        past = lane < n_past
        cur = jnp.where(past, gate, NEG_INF)
        lidx = lane.astype(F32)
        sel = jnp.zeros(gate.shape, F32)
        for _ in range(MOBA_TOPK):
            mx = jnp.max(cur, axis=1, keepdims=True)
            first = jnp.min(jnp.where(cur == mx, lidx, float(LANES)), axis=1, keepdims=True)
            hit = lidx == first
            sel = jnp.where(hit, 1.0, sel)
            cur = jnp.where(hit, -jnp.inf, cur)
        sel_ref[0] = jnp.where(past, sel, 0.0)


def _decode_sel(att, cos, sin, qg, kg, cache_kt, page_table, layer, pages_per_step=8):
    db, t, _ = att.shape
    n_pages = page_table.shape[1]
    n_steps = n_pages // pages_per_step
    rows = N_HEADS * t
    assert n_pages * PAGE // MOBA_BLOCK <= LANES

    def page_map(i):
        return lambda b, s, pt: (layer, pt[b, s * pages_per_step + i], 0, 0, 0)

    grid_spec = pltpu.PrefetchScalarGridSpec(
        num_scalar_prefetch=1,
        grid=(db, n_steps),
        in_specs=[pl.BlockSpec((1, t, 3 * GROUP), lambda b, s, pt: (b, 0, 0)),
                  pl.BlockSpec((t, GROUP), lambda b, s, pt: (0, 0)),
                  pl.BlockSpec((t, GROUP), lambda b, s, pt: (0, 0)),
                  pl.BlockSpec((1, GROUP), lambda b, s, pt: (0, 0)),
                  pl.BlockSpec((1, GROUP), lambda b, s, pt: (0, 0))]
                 + [pl.BlockSpec((1, 1, N_HEADS, HEAD_DIM, PAGE), page_map(i)) for i in range(pages_per_step)],
        out_specs=[pl.BlockSpec((1, rows, LANES), lambda b, s, pt: (b, 0, 0)),
                   pl.BlockSpec((1, t, GROUP), lambda b, s, pt: (b, 0, 0)),
                   pl.BlockSpec((1, t, GROUP), lambda b, s, pt: (b, 0, 0))],
        scratch_shapes=[pltpu.VMEM((2 * rows, GROUP), BF16),
                        pltpu.VMEM((rows, LANES), F32)],
    )
    return pl.pallas_call(
        functools.partial(_decode_sel_kernel, pages_per_step=pages_per_step),
        grid_spec=grid_spec,
        out_shape=[jax.ShapeDtypeStruct((db, N_HEADS * t, LANES), F32),
                   jax.ShapeDtypeStruct((db, t, GROUP), F32),
                   jax.ShapeDtypeStruct((db, t, GROUP), F32)],
        compiler_params=_cparams(2),
        name="decode_sel",
    )(page_table, att, cos, sin, qg, kg, *([cache_kt] * pages_per_step))


def _decode_att_kernel(pt_ref, sel_ref, qr_ref, kn_ref, vn_ref, *rest, pages_per_step):
    k_refs = rest[:pages_per_step]
    v_refs = rest[pages_per_step:2 * pages_per_step]
    oa_ref, qbd_ref, acc_ref, m_ref, l_ref = rest[2 * pages_per_step:]
    j = pl.program_id(1)
    n_steps = pl.num_programs(1)
    t = qr_ref.shape[1]
    blocks_per_step = pages_per_step * PAGE // MOBA_BLOCK

    @pl.when(j == 0)
    def _():
        qbd_ref[...] = _stack_heads(qr_ref[0] * (HEAD_DIM ** -0.5)).astype(BF16)
        pad = jnp.zeros((MOBA_BLOCK - t, GROUP), F32)
        kn = jnp.concatenate([kn_ref[0], pad], axis=0).astype(BF16)
        vn = jnp.concatenate([vn_ref[0], pad], axis=0).astype(BF16)
        st = lax.dot_general(qbd_ref[...], kn, (((1,), (1,)), ((), ())), preferred_element_type=F32)
        qpos = lax.broadcasted_iota(jnp.int32, st.shape, 0) % t
        kpos = lax.broadcasted_iota(jnp.int32, st.shape, 1)
        st = jnp.where(kpos <= qpos, st, NEG_INF)
        m = jnp.max(st, axis=1, keepdims=True)
        p = jnp.exp(st - m)
        m_ref[...] = m
        l_ref[...] = jnp.sum(p, axis=1, keepdims=True)
        acc_ref[...] = jnp.dot(p.astype(BF16), vn, preferred_element_type=F32)

    qbd = qbd_ref[...]
    pages_per_block = MOBA_BLOCK // PAGE
    sel = sel_ref[0]
    lane = lax.broadcasted_iota(jnp.int32, sel.shape, 1)
    masked = []
    for pi, r in enumerate(k_refs):
        sc = jnp.dot(qbd, r[0, 0].reshape(GROUP, PAGE).astype(BF16), preferred_element_type=F32)
        blk = j * blocks_per_step + pi // pages_per_block
        chosen = jnp.sum(jnp.where(lane == blk, sel, 0.0), axis=1, keepdims=True) > 0.0
        masked.append(jnp.where(chosen, sc, NEG_INF))
    st = jnp.concatenate(masked, axis=1)
    m_old = m_ref[...]
    m_new = jnp.maximum(m_old, jnp.max(st, axis=1, keepdims=True))
    alpha = jnp.exp(m_old - m_new)
    p32 = jnp.exp(st - m_new)
    p = p32.astype(BF16)
    m_ref[...] = m_new
    l_ref[...] = alpha * l_ref[...] + jnp.sum(p32, axis=1, keepdims=True)
    acc = alpha * acc_ref[...]
    for pi, r in enumerate(v_refs):
        acc = acc + lax.dot_general(p[:, pi * PAGE:(pi + 1) * PAGE], r[0, 0].reshape(GROUP, PAGE).astype(BF16),
                                    (((1,), (1,)), ((), ())), preferred_element_type=F32)
    acc_ref[...] = acc

    @pl.when(j == n_steps - 1)
    def _():
        o = acc_ref[...] / l_ref[...]
        lane_head = _head_of_lane((t, GROUP), 1)
        out = jnp.zeros((t, GROUP), F32)
        for h in range(N_HEADS):
            out = out + jnp.where(lane_head == h, o[h * t:(h + 1) * t, :], 0.0)
        oa_ref[0] = out


def _decode_att(sel, qr, kn, vn, cache_kt, cache_vt, page_table, layer, pages_per_step=8):
    db, t, _ = qr.shape
    n_steps = page_table.shape[1] // pages_per_step
    rows = N_HEADS * t

    def page_map(i):
        return lambda b, j, pt: (layer, pt[b, pages_per_step * j + i], 0, 0, 0)

    pages = [pl.BlockSpec((1, 1, N_HEADS, HEAD_DIM, PAGE), page_map(i)) for i in range(pages_per_step)]
    grid_spec = pltpu.PrefetchScalarGridSpec(
        num_scalar_prefetch=1,
        grid=(db, n_steps),
        in_specs=[pl.BlockSpec((1, rows, LANES), lambda b, j, pt: (b, 0, 0)),
                  pl.BlockSpec((1, t, GROUP), lambda b, j, pt: (b, 0, 0)),
                  pl.BlockSpec((1, t, GROUP), lambda b, j, pt: (b, 0, 0)),
                  pl.BlockSpec((1, t, GROUP), lambda b, j, pt: (b, 0, 0))] + pages + pages,
        out_specs=[pl.BlockSpec((1, t, GROUP), lambda b, j, pt: (b, 0, 0))],
        scratch_shapes=[pltpu.VMEM((rows, GROUP), BF16),
                        pltpu.VMEM((rows, GROUP), F32),
                        pltpu.VMEM((rows, 1), F32),
                        pltpu.VMEM((rows, 1), F32)],
    )
    return pl.pallas_call(
        functools.partial(_decode_att_kernel, pages_per_step=pages_per_step),
        grid_spec=grid_spec,
        out_shape=[jax.ShapeDtypeStruct((db, t, GROUP), F32)],
        compiler_params=_cparams(2),
        name="decode_att",
    )(page_table, sel, qr, kn, vn, *([cache_kt] * pages_per_step), *([cache_vt] * pages_per_step))[0]


def _conv_kernel(conf_ref, sc_ref, cbuf_ref, sbuf_ref, dw_ref, dwb_ref, lng_ref, lnb_ref, scw_ref,
                 ob_ref, od_ref, cnew_ref, snew_ref, extc_ref, exts_ref):
    ti = pl.program_id(1)
    nt = pl.num_programs(1)
    tt = conf_ref.shape[1]
    hc = 32
    hs = 8
    nbc = CONF_TAPS - 1
    nbs = SC_TAPS - 1

    @pl.when(ti == 0)
    def _():
        extc_ref[0:hc - nbc, :] = jnp.zeros((hc - nbc, GROUP), F32)
        extc_ref[hc - nbc:hc, :] = cbuf_ref[0]
        exts_ref[0:hs - nbs, :] = jnp.zeros((hs - nbs, GROUP), F32)
        exts_ref[hs - nbs:hs, :] = sbuf_ref[0]

    c = conf_ref[0]
    glu = c[:, 0:GROUP] * _sigmoid(c[:, GROUP:2 * GROUP])
    extc_ref[hc:hc + tt, :] = glu
    acc = jnp.zeros((tt, GROUP), F32) + dwb_ref[...]
    for j in range(CONF_TAPS):
        acc = acc + dw_ref[j:j + 1, :] * extc_ref[pl.ds(hc - nbc + j, tt), :]
    mu = jnp.mean(acc, axis=-1, keepdims=True)
    var = jnp.mean(jnp.square(acc - mu), axis=-1, keepdims=True)
    ln = (acc - mu) * lax.rsqrt(var + LN_EPS) * lng_ref[...] + lnb_ref[...]
    ob_ref[0] = ln * _sigmoid(ln)

    s = sc_ref[0]
    gated = s[:, GROUP:2 * GROUP] * s[:, 2 * GROUP:3 * GROUP]
    exts_ref[hs:hs + tt, :] = gated
    d = jnp.zeros((tt, GROUP), F32)
    for j in range(SC_TAPS):
        d = d + scw_ref[j:j + 1, :] * exts_ref[pl.ds(hs - nbs + j, tt), :]
    od_ref[0] = s[:, 0:GROUP] * d

    @pl.when(ti == nt - 1)
    def _():
        cnew_ref[0] = extc_ref[pl.ds(tt + hc - nbc, nbc), :]
        snew_ref[0] = exts_ref[pl.ds(tt + hs - nbs, nbs), :]

    tail_c = extc_ref[pl.ds(tt, hc), :]
    tail_s = exts_ref[pl.ds(tt, hs), :]
    extc_ref[0:hc, :] = tail_c
    exts_ref[0:hs, :] = tail_s


def _conv(conf, sc, cbuf, sbuf, dw, dwb, lng, lnb, scw, tt):
    b, t, _ = conf.shape
    nbc = CONF_TAPS - 1
    nbs = SC_TAPS - 1
    const = lambda i, j: (0, 0)
    return pl.pallas_call(
        _conv_kernel,
        grid=(b, t // tt),
        in_specs=[pl.BlockSpec((1, tt, 2 * GROUP), lambda i, j: (i, j, 0)),
                  pl.BlockSpec((1, tt, 3 * GROUP), lambda i, j: (i, j, 0)),
                  pl.BlockSpec((1, nbc, GROUP), lambda i, j: (i, 0, 0)),
                  pl.BlockSpec((1, nbs, GROUP), lambda i, j: (i, 0, 0)),
                  pl.BlockSpec((CONF_TAPS, GROUP), const),
                  pl.BlockSpec((1, GROUP), const),
                  pl.BlockSpec((1, GROUP), const),
                  pl.BlockSpec((1, GROUP), const),
                  pl.BlockSpec((SC_TAPS, GROUP), const)],
        out_specs=[pl.BlockSpec((1, tt, GROUP), lambda i, j: (i, j, 0)),
                   pl.BlockSpec((1, tt, GROUP), lambda i, j: (i, j, 0)),
                   pl.BlockSpec((1, nbc, GROUP), lambda i, j: (i, 0, 0)),
                   pl.BlockSpec((1, nbs, GROUP), lambda i, j: (i, 0, 0))],
        out_shape=[jax.ShapeDtypeStruct((b, t, GROUP), F32),
                   jax.ShapeDtypeStruct((b, t, GROUP), F32),
                   jax.ShapeDtypeStruct((b, nbc, GROUP), F32),
                   jax.ShapeDtypeStruct((b, nbs, GROUP), F32)],
        scratch_shapes=[pltpu.VMEM((32 + tt, GROUP), F32),
                        pltpu.VMEM((8 + tt, GROUP), F32)],
        compiler_params=_cparams(2),
        name="conv",
    )(conf, sc, cbuf, sbuf, dw, dwb, lng, lnb, scw)


def _rwkv_kernel(rw_ref, shift_ref, st0_ref, mu_ref, w0_ref, a0_ref, kkw_ref, ka_ref, rk_ref,
                 lng_ref, lnb_ref, w2_ref, a2_ref, g2_ref, oc_ref, sto_ref, st_ref, prev_ref):
    c = pl.program_id(1)

    @pl.when(c == 0)
    def _():
        st_ref[...] = st0_ref[...]
        prev_ref[...] = shift_ref[...]

    nseq, L, _ = rw_ref.shape
    SL = max(N_HEADS * L, LANES)
    seqs = range(nseq)
    each = lambda f, *cols: [f(*args) for args in zip(*cols)]
    P = [_rwkv_prep(s, rw_ref, mu_ref, w0_ref, a0_ref, kkw_ref, ka_ref, w2_ref, a2_ref, g2_ref, prev_ref)
         for s in seqs]
    at_s, bt_s, kt_s, rt_s, bg_s, kg_s, v_s = ([p[i] for p in P] for i in range(7))

    srow = lax.broadcasted_iota(jnp.int32, (SL, SL), 0)
    scol = lax.broadcasted_iota(jnp.int32, (SL, SL), 1)
    same = (srow // L) == (scol // L)
    strict = same & ((srow % L) > (scol % L))
    incl = same & ((srow % L) >= (scol % L))
    eye = jnp.where(srow == scol, 1.0, 0.0)
    n_ab = each(lambda a, b: jnp.where(strict, _mm_nt(a, b), 0.0), at_s, bt_s)
    n_ak = each(lambda a, b: jnp.where(strict, _mm_nt(a, b), 0.0), at_s, kt_s)
    q_rb = each(lambda a, b: jnp.where(incl, _mm_nt(a, b), 0.0), rt_s, bt_s)
    q_rk = each(lambda a, b: jnp.where(incl, _mm_nt(a, b), 0.0), rt_s, kt_s)

    tm = [eye + n for n in n_ab]
    pw = n_ab
    for _ in range(int(math.log2(L)) - 1):
        pw = each(_mm, pw, pw)
        tm = each(lambda t, x: t + _mm(t, x), tm, pw)

    p1 = each(_mm, tm, at_s)
    p2 = each(_mm, tm, each(_mm, n_ak, v_s))
    bg_t = [u.T for u in bg_s]
    kg_t = [u.T for u in kg_s]
    drow = lax.broadcasted_iota(jnp.int32, (GROUP, GROUP), 0)
    dcol = lax.broadcasted_iota(jnp.int32, (GROUP, GROUP), 1)
    m_mat = each(lambda p, bt, x: jnp.where(drow == dcol, jnp.exp(p[7]), 0.0) + _mm3(bt, x), P, bg_t, p1)
    c_mat = each(lambda bt, x, kt, vs: _mm3(bt, x) + _mm3(kt, vs), bg_t, p2, kg_t, v_s)
    yc = each(lambda rt, q, x: rt + _mm(q, x), rt_s, q_rb, p1)
    y0 = each(lambda q, x, qk, vs: _mm(q, x) + _mm(qk, vs), q_rb, p2, q_rk, v_s)

    st = [st_ref[s] for s in seqs]
    y_s = each(lambda a, b, y: _mm3(a, b) + y, yc, st, y0)
    st_new = each(lambda m, b, cm: _mm3(m, b) + cm, m_mat, st, c_mat)
    bd = _head_sum_matrix()
    inv_n = 1.0 / HEAD_DIM
    for s in seqs:
        st_ref[s] = st_new[s]
        y = y_s[s][0:L, :]
        for h in range(1, N_HEADS):
            y = y + y_s[s][h * L:(h + 1) * L, :]
        r, k2, v, g = P[s][8:12]
        mean = _mm2_exact_rhs(y, bd) * inv_n
        yc0 = y - mean
        var = _mm2_exact_rhs(yc0 * yc0, bd) * inv_n
        yn = yc0 * lax.rsqrt(var + RWKV_GN_EPS) * lng_ref[...] + lnb_ref[...]
        bonus = _mm2_exact_rhs(r * k2 * rk_ref[...], bd) * v
        oc_ref[s] = (yn + bonus) * g

    @pl.when(c == pl.num_programs(1) - 1)
    def _():
        sto_ref[...] = st_ref[...]


def _rwkv_prep(s, rw_ref, mu_ref, w0_ref, a0_ref, kkw_ref, ka_ref, w2_ref, a2_ref, g2_ref, prev_ref):
    L = rw_ref.shape[1]
    SL = max(N_HEADS * L, LANES)
    bd = _head_sum_matrix()

    z = rw_ref[s]
    row = lax.broadcasted_iota(jnp.int32, z.shape, 0)
    zprev = jnp.where(row == 0, prev_ref[s], pltpu.roll(z, 1, axis=0))
    prev_ref[s] = z[L - 1:L, :]
    zs = z + (zprev - z) * mu_ref[...]
    r = zs[:, 0:GROUP]
    k = zs[:, GROUP:2 * GROUP]
    v = zs[:, 2 * GROUP:3 * GROUP]
    low = zs[:, 3 * GROUP:3 * GROUP + RWKV_LOW]

    x = -(w0_ref[...] + _mm3(jnp.tanh(low), w2_ref[...]))
    softplus = jnp.maximum(x, 0.0) + jnp.log(1.0 + jnp.exp(-jnp.abs(x)))
    ld = -jnp.exp(-softplus - 0.5)
    a = _sigmoid(a0_ref[...] + _mm3(low, a2_ref[...]))
    g = _mm(_sigmoid(low), g2_ref[...])
    kk = k * kkw_ref[...]
    kk = kk / jnp.maximum(jnp.sqrt(_mm2_exact_rhs(kk * kk, bd)), 1e-12)
    k2 = k * (1.0 + (a - 1.0) * ka_ref[...])

    trow = lax.broadcasted_iota(jnp.int32, (L, L), 0)
    tcol = lax.broadcasted_iota(jnp.int32, (L, L), 1)
    tri = jnp.where(trow >= tcol, 1.0, 0.0).astype(BF16)
    cum = _mm2_exact_rhs_lhs(tri, ld)
    cend = cum[L - 1:L, :]
    g_prev = jnp.exp(cum - ld)
    g_inv = jnp.exp(-cum)
    g_rel = jnp.exp(cend - cum)
    kka = kk * a

    lane_head = _head_of_lane((L, GROUP), 1)

    def stack(u):
        parts = [jnp.where(lane_head == h, u, 0.0) for h in range(N_HEADS)]
        if SL > N_HEADS * L:
            parts.append(jnp.zeros((SL - N_HEADS * L, GROUP), F32))
        return jnp.concatenate(parts, axis=0)

    at_s = stack(-kk * g_prev)
    bt_s = stack(kka * g_inv)
    kt_s = stack(k2 * g_inv)
    rt_s = stack(r * jnp.exp(cum))
    bg_s = stack(kka * g_rel)
    kg_s = stack(k2 * g_rel)
    v_s = stack(v)
    return at_s, bt_s, kt_s, rt_s, bg_s, kg_s, v_s, cend, r, k2, v, g


def _mm2_exact_rhs_lhs(a_bf16, b):
    bh, bl = _split(b)
    d = functools.partial(jnp.dot, preferred_element_type=F32)
    return d(a_bf16, bh) + d(a_bf16, bl)


def _rwkv(rw, shift, st0, p, L, nseq=2):
    b, t, w = rw.shape
    const = lambda i, j: (0, 0)
    vec = pl.BlockSpec((1, GROUP), const)
    low = pl.BlockSpec((RWKV_LOW, GROUP), const)
    return pl.pallas_call(
        _rwkv_kernel,
        grid=(b // nseq, t // L),
        in_specs=[pl.BlockSpec((nseq, L, w), lambda i, j: (i, j, 0)),
                  pl.BlockSpec((nseq, 1, w), lambda i, j: (i, 0, 0)),
                  pl.BlockSpec((nseq, GROUP, GROUP), lambda i, j: (i, 0, 0)),
                  pl.BlockSpec((1, w), const),
                  vec, vec, vec, vec, vec, vec, vec, low, low, low],
        out_specs=[pl.BlockSpec((nseq, L, GROUP), lambda i, j: (i, j, 0)),
                   pl.BlockSpec((nseq, GROUP, GROUP), lambda i, j: (i, 0, 0))],
        out_shape=[jax.ShapeDtypeStruct((b, t, GROUP), F32),
                   jax.ShapeDtypeStruct((b, GROUP, GROUP), F32)],
        scratch_shapes=[pltpu.VMEM((nseq, GROUP, GROUP), F32),
                        pltpu.VMEM((nseq, 1, w), F32)],
        compiler_params=_cparams(2),
        name="rwkv",
    )(rw, shift, st0, p["mu"], p["w0"], p["a0"], p["kkw"], p["ka"], p["rk"], p["lng"], p["lnb"],
      p["w2"], p["a2"], p["g2"])


def _state_to_blockdiag(s):
    b = s.shape[0]
    st = jnp.swapaxes(s, -1, -2)
    eye = jnp.eye(N_HEADS, dtype=s.dtype)
    return jnp.einsum("bhkv,hg->bhkgv", st, eye).reshape(b, GROUP, GROUP)


def _blockdiag_to_state(st):
    b = st.shape[0]
    x = st.reshape(b, N_HEADS, HEAD_DIM, N_HEADS, HEAD_DIM)
    diag = jnp.stack([x[:, h, :, h, :] for h in range(N_HEADS)], axis=1)
    return jnp.swapaxes(diag, -1, -2)


def _outproj_kernel(oa_ref, ob_ref, oc_ref, od_ref, x_ref, wo_ref, g_ref, wq_ref, x1_ref, h_ref, q_ref):
    acc = x_ref[...]
    for i, o_ref in enumerate((oa_ref, ob_ref, oc_ref, od_ref)):
        acc = acc + jnp.dot(o_ref[...].astype(BF16), wo_ref[i], preferred_element_type=F32)
    x1_ref[...] = acc
    ms = jnp.mean(acc * acc, axis=-1, keepdims=True)
    h = (acc * lax.rsqrt(ms + RMS_EPS) * g_ref[...]).astype(BF16)
    h_ref[...] = h
    q = jnp.dot(h, wq_ref[...], preferred_element_type=F32)
    qd = q_ref.shape[2]
    for hd in range(q_ref.shape[0]):
        q_ref[hd] = q[:, hd * qd:(hd + 1) * qd]


def _outproj(oa, ob, oc, od, x, wo, g, wq, tm):
    n, d = x.shape
    dq = wq.shape[1]
    mix = pl.BlockSpec((tm, GROUP), lambda i: (i, 0))
    return pl.pallas_call(
        _outproj_kernel,
        grid=(n // tm,),
        in_specs=[mix, mix, mix, mix,
                  pl.BlockSpec((tm, d), lambda i: (i, 0)),
                  pl.BlockSpec((4, GROUP, d), lambda i: (0, 0, 0)),
                  pl.BlockSpec((1, d), lambda i: (0, 0)),
                  pl.BlockSpec((d, dq), lambda i: (0, 0))],
        out_specs=[pl.BlockSpec((tm, d), lambda i: (i, 0)),
                   pl.BlockSpec((tm, d), lambda i: (i, 0)),
                   pl.BlockSpec((PEER_HEADS, tm, dq // PEER_HEADS), lambda i: (0, i, 0))],
        out_shape=[jax.ShapeDtypeStruct((n, d), F32),
                   jax.ShapeDtypeStruct((n, d), BF16),
                   jax.ShapeDtypeStruct((PEER_HEADS, n, dq // PEER_HEADS), F32)],
        compiler_params=_cparams(1),
        name="outproj",
    )(oa, ob, oc, od, x, wo, g, wq)


_CAND_PIECES = ((0, 0, 8), (0, 8, 8), (1, 0, 8), (2, 0, 5), (3, 0, 4), (4, 0, 3), (5, 0, 2),
                (6, 0, 2), (7, 0, 2))
_CAND_ROWS = 8 * (len(_CAND_PIECES) + 1)


def _sort_network(n):
    size = 1
    while size < n:
        size *= 2
    pairs = []
    p = 1
    while p < size:
        k = p
        while k >= 1:
            for j in range(k % p, size - k, 2 * k):
                for i in range(min(k, size - j - k)):
                    if (i + j) // (2 * p) == (i + j + k) // (2 * p):
                        pairs.append((i + j, i + j + k))
            k //= 2
        p *= 2
    return tuple((i, j) for i, j in pairs if j < n)


_SUBLANES = 8


def _top_values(x, depth):
    groups = x.shape[0] // _SUBLANES
    v = [x[_SUBLANES * k:_SUBLANES * (k + 1), :] for k in range(groups)]
    for i, j in _sort_network(groups):
        v[i], v[j] = jnp.maximum(v[i], v[j]), jnp.minimum(v[i], v[j])
    sidx = lax.broadcasted_iota(jnp.int32, v[0].shape, 0).astype(F32)
    out = []
    for r in range(depth):
        mx = jnp.max(v[0], axis=0, keepdims=True)
        first = jnp.min(jnp.where(v[0] == mx, sidx, float(_SUBLANES)), axis=0, keepdims=True)
        pop = sidx == first
        out.append(mx)
        reachable = depth - 1 - r
        for k in range(min(groups - 1, reachable)):
            v[k] = jnp.where(pop, v[k + 1], v[k])
        if reachable > groups - 1:
            v[groups - 1] = jnp.where(pop, -jnp.inf, v[groups - 1])
    return jnp.concatenate(out, axis=0)


_SELECT_TILES_PER_ITER = 2


def _peer_select_kernel(q_ref, k1_ref, k2_ref, s1_ref, s2_ref, thr_ref, stat_ref):
    tn = q_ref.shape[1]
    n_lt = tn // LANES
    q = q_ref[0].astype(BF16)
    nt = functools.partial(lax.dot_general, dimension_numbers=(((1,), (1,)), ((), ())),
                           preferred_element_type=F32)
    s1_all = nt(k1_ref[...], q)
    s2_all = nt(k2_ref[...], q)
    for lt in range(n_lt):
        s1_ref[lt, 0] = s1_all[:, lt * LANES:(lt + 1) * LANES]
        s2_ref[lt, 0] = s2_all[:, lt * LANES:(lt + 1) * LANES]

    sub = lax.broadcasted_iota(jnp.int32, (8, LANES), 0)

    def one_tile(lt):
        s1 = s1_ref[lt, 0]
        v1 = _top_values(s1, PEER_TOPK)
        v2 = _top_values(s2_ref[lt, 0], PEER_TOPK)
        pieces = [jnp.where(sub < nb, v1[a:a + 1, :] + v2[b0:b0 + 8, :], -jnp.inf)
                  for a, b0, nb in _CAND_PIECES]
        pieces.append(v1[8:16, :] + v2[0:1, :])
        cv = _top_values(jnp.concatenate(pieces, axis=0), PEER_TOPK)
        tau = cv[PEER_TOPK - 1:PEER_TOPK, :]
        z = jnp.sum(jnp.exp(cv - cv[0:1, :]), axis=0, keepdims=True)
        stat_ref[lt, 0] = jnp.concatenate(
            [tau, v1[0:1, :], v2[0:1, :], 1.0 / z, jnp.zeros((4, LANES), F32)], axis=0)
        thr = jnp.full((PEER_NKEYS, LANES), jnp.inf, F32)
        for b in range(PEER_TOPK):
            vb = v2[b:b + 1, :]
            thr = jnp.where((s1 + vb) >= tau, vb, thr)
        thr_ref[lt, 0] = thr

    per_iter = math.gcd(n_lt, _SELECT_TILES_PER_ITER)

    def tile_body(i, carry):
        for u in range(per_iter):
            one_tile(i * per_iter + u)
        return carry

    lax.fori_loop(0, n_lt // per_iter, tile_body, 0)


def _peer_select(q, k1p, k2p, tn):
    _, n, qd = q.shape
    n_lt = tn // LANES
    keyed = pl.BlockSpec((n_lt, 1, PEER_NKEYS, LANES), lambda i, h: (i, h, 0, 0))
    keyed_shape = jax.ShapeDtypeStruct((n // LANES, PEER_HEADS, PEER_NKEYS, LANES), F32)
    return pl.pallas_call(
        _peer_select_kernel,
        grid=(n // tn, PEER_HEADS),
        in_specs=[pl.BlockSpec((1, tn, qd), lambda i, h: (h, i, 0)),
                  pl.BlockSpec((PEER_NKEYS, qd), lambda i, h: (0, 0)),
                  pl.BlockSpec((PEER_NKEYS, qd), lambda i, h: (0, 0))],
        out_specs=[keyed, keyed, keyed,
                   pl.BlockSpec((n_lt, 1, 8, LANES), lambda i, h: (i, h, 0, 0))],
        out_shape=[keyed_shape, keyed_shape, keyed_shape,
                   jax.ShapeDtypeStruct((n // LANES, PEER_HEADS, 8, LANES), F32)],
        compiler_params=_cparams(2),
        name="peer_select",
    )(q, k1p, k2p)


_PEER_SUB_KEYS = 2


def _peer_dense_kernel(h_ref, *rest, n_sub):
    u_refs = rest[:n_sub]
    vt_refs = rest[n_sub:2 * n_sub]
    s1_ref, s2_ref, thr_ref, stat_ref, x1_ref, y_ref, acc_ref, e1_ref, e2_ref = rest[2 * n_sub:]
    j = pl.program_id(1)
    nj = pl.num_programs(1)
    n_lt = s2_ref.shape[0]
    ne1 = n_sub * _PEER_SUB_KEYS

    @pl.when(j == 0)
    def _():
        acc_ref[...] = jnp.zeros(acc_ref.shape, F32)
        for lt in range(n_lt):
            for h in range(PEER_HEADS):
                st = stat_ref[lt, h]
                e1_ref[lt, h] = jnp.exp(s1_ref[lt, h] - st[1:2, :]) * st[3:4, :]
                e2_ref[lt, h] = jnp.exp(s2_ref[lt, h] - st[2:3, :])

    hx = h_ref[...]
    dep = jnp.zeros((1, LANES), F32)

    def activations(sb):
        return lax.dot_general(u_refs[sb][...], hx, (((1,), (1,)), ((), ())), preferred_element_type=F32)

    act_next = activations(0)
    for sb in range(n_sub):
        act = act_next
        if sb + 1 < n_sub:
            act_next = activations(sb + 1)
        w_rows = []
        for e in range(_PEER_SUB_KEYS):
            e1 = j * ne1 + sb * _PEER_SUB_KEYS + e
            w_tiles = []
            for lt in range(n_lt):
                gate = jnp.zeros((PEER_NKEYS, LANES), F32)
                for h in range(PEER_HEADS):
                    chosen = s2_ref[lt, h] >= (thr_ref[lt, h, pl.ds(e1, 1), :] + dep)
                    gate = gate + jnp.where(chosen, e2_ref[lt, h], 0.0) * e1_ref[lt, h, pl.ds(e1, 1), :]
                a = act[e * PEER_NKEYS:(e + 1) * PEER_NKEYS, lt * LANES:(lt + 1) * LANES]
                wt = gate * (0.5 * a * (1.0 + lax.erf(a * (2.0 ** -0.5))))
                bits = pltpu.bitcast(wt[0:1, :], jnp.uint32)
                dep = pltpu.bitcast(lax.shift_right_logical(lax.shift_right_logical(bits, jnp.uint32(16)),
                                                            jnp.uint32(16)), F32)
                w_tiles.append(wt.astype(BF16))
            w_rows.append(jnp.concatenate(w_tiles, axis=1))
        w = jnp.concatenate(w_rows, axis=0)
        acc_ref[...] += jnp.dot(vt_refs[sb][0], w, preferred_element_type=F32)

    @pl.when(j == nj - 1)
    def _():
        y_ref[...] = x1_ref[...] + acc_ref[...].T


def _peer_dense(h, u, vt, s1, s2, thr, stat, x1, tn, eb):
    n, d = x1.shape
    ne = u.shape[0]
    n_lt = tn // LANES
    sub_rows = _PEER_SUB_KEYS * PEER_NKEYS
    n_sub = eb // sub_rows
    assert vt.shape == (ne // sub_rows, d, sub_rows)
    keyed = pl.BlockSpec((n_lt, PEER_HEADS, PEER_NKEYS, LANES), lambda i, j: (i, 0, 0, 0))
    u_specs = [pl.BlockSpec((sub_rows, d), lambda i, j, k=k: (j * n_sub + k, 0)) for k in range(n_sub)]
    vt_specs = [pl.BlockSpec((1, d, sub_rows), lambda i, j, k=k: (j * n_sub + k, 0, 0)) for k in range(n_sub)]
    return pl.pallas_call(
        functools.partial(_peer_dense_kernel, n_sub=n_sub),
        grid=(n // tn, ne // eb),
        in_specs=[pl.BlockSpec((tn, d), lambda i, j: (i, 0))] + u_specs + vt_specs + [
                  keyed, keyed, keyed,
                  pl.BlockSpec((n_lt, PEER_HEADS, 8, LANES), lambda i, j: (i, 0, 0, 0)),
                  pl.BlockSpec((tn, d), lambda i, j: (i, 0))],
        out_specs=[pl.BlockSpec((tn, d), lambda i, j: (i, 0))],
        out_shape=[jax.ShapeDtypeStruct((n, d), F32)],
        scratch_shapes=[pltpu.VMEM((d, tn), F32),
                        pltpu.VMEM((n_lt, PEER_HEADS, PEER_NKEYS, LANES), F32),
                        pltpu.VMEM((n_lt, PEER_HEADS, PEER_NKEYS, LANES), F32)],
        compiler_params=_cparams(2),
        name="peer_dense",
    )(h, *([u] * n_sub), *([vt] * n_sub), s1, s2, thr, stat, x1)[0]


def _rope_tables(pos):
    half = HEAD_DIM // 2
    inv_freq = ROPE_THETA ** (-jnp.arange(half, dtype=F32) / half)
    ang = pos.astype(F32)[:, None] * inv_freq[None, :]
    cos = jnp.tile(jnp.cos(ang), (1, 2 * N_HEADS))
    sin = jnp.sin(ang)
    sin_signed = jnp.tile(jnp.concatenate([-sin, sin], axis=1), (1, N_HEADS))
    return cos, sin_signed


def _row_tile(n, pref):
    return pref if n % pref == 0 else n


def _seq_group(b, pref):
    return math.gcd(b, pref)


def _layer_params(l, norm_mix, w_in, q_norm, k_norm, conf_dw, conf_dw_b, conf_ln_g, conf_ln_b,
                  rwkv_mu, rwkv_w0, rwkv_w2, rwkv_a0, rwkv_a2, rwkv_g2, rwkv_kk, rwkv_ka, rwkv_rk,
                  rwkv_ln_g, rwkv_ln_b, sc_w, w_out, norm_ffn, peer_wq, peer_k1, peer_k2, peer_u, peer_v):
    row = lambda a: a[l].reshape(1, -1)
    att_c = 3 * GROUP
    conf_c = 2 * GROUP
    rw_c = rwkv_mu.shape[1]
    wi = w_in[l].astype(BF16)
    bounds = [0, att_c, att_c + conf_c, att_c + conf_c + rw_c, wi.shape[1]]
    rd, ra, rg = rwkv_w2.shape[1], rwkv_a2.shape[1], rwkv_g2.shape[1]
    zeros = lambda r: jnp.zeros((r, GROUP), F32)
    half = peer_k1.shape[2]
    return dict(
        norm_mix=row(norm_mix),
        w_in=[wi[:, bounds[i]:bounds[i + 1]] for i in range(4)],
        qg=jnp.tile(row(q_norm), (1, N_HEADS)), kg=jnp.tile(row(k_norm), (1, N_HEADS)),
        dw=conf_dw[l], dwb=row(conf_dw_b), clng=row(conf_ln_g), clnb=row(conf_ln_b), scw=sc_w[l],
        rwkv=dict(mu=row(rwkv_mu), w0=row(rwkv_w0), a0=row(rwkv_a0), kkw=row(rwkv_kk), ka=row(rwkv_ka),
                  rk=row(rwkv_rk), lng=row(rwkv_ln_g), lnb=row(rwkv_ln_b),
                  w2=jnp.concatenate([rwkv_w2[l], zeros(ra + rg)], axis=0),
                  a2=jnp.concatenate([zeros(rd), rwkv_a2[l], zeros(rg)], axis=0),
                  g2=jnp.concatenate([zeros(rd + ra), rwkv_g2[l]], axis=0)),
        w_out=w_out[l].astype(BF16).reshape(4, GROUP, -1),
        norm_ffn=row(norm_ffn), wq=peer_wq[l].astype(BF16),
        k1p=jnp.concatenate([peer_k1[l], jnp.zeros_like(peer_k1[l])], axis=1).astype(BF16),
        k2p=jnp.concatenate([jnp.zeros_like(peer_k2[l]), peer_k2[l]], axis=1).astype(BF16),
        u=peer_u[l].astype(BF16),
        vt=jnp.swapaxes(peer_v[l].astype(BF16).reshape(-1, _PEER_SUB_KEYS * PEER_NKEYS, peer_v.shape[2]), 1, 2),
    )


def _mix_and_ffn(x, oa, ob, oc, od, p):
    b, t, d = x.shape
    n = b * t
    flat = lambda a: a.reshape(n, a.shape[-1])
    tm = _row_tile(n, 512)
    x1, h2, q = _outproj(flat(oa), flat(ob), flat(oc), flat(od), flat(x), p["w_out"], p["norm_ffn"],
                         p["wq"], tm)
    tn = _row_tile(n, 512)
    s1, s2, thr, stat = _peer_select(q, p["k1p"], p["k2p"], tn)
    y = _peer_dense(h2, p["u"], p["vt"], s1, s2, thr, stat, x1, tn, PEER_EXPERT_BLOCK)
    return y.reshape(b, t, d)


def _common_mixers(x, p, conf_buf, shift_prev, wkv_prev, sc_buf, chunk, nseq, conv_tile):
    b, t, d = x.shape
    n = b * t
    att, conf, rw, sc = _proj(x.reshape(n, d), p["norm_mix"], p["w_in"], _row_tile(n, 512))
    att = att.reshape(b, t, -1)
    conf = conf.reshape(b, t, -1)
    rw = rw.reshape(b, t, -1)
    sc = sc.reshape(b, t, -1)
    ob, od, conf_new, sc_new = _conv(conf, sc, conf_buf, sc_buf, p["dw"], p["dwb"], p["clng"], p["clnb"],
                                     p["scw"], conv_tile)
    oc, st_out = _rwkv(rw, shift_prev[:, None, :], _state_to_blockdiag(wkv_prev), p["rwkv"], chunk, nseq)
    return att, ob, oc, od, conf_new, rw[:, -1, :], _blockdiag_to_state(st_out), sc_new


def kernel(x_prompt, x_sample, cache_k, cache_v, page_table, state_conf_conv, state_rwkv_shift, state_rwkv_wkv, state_short_conv, norm_mix, w_in, q_norm, k_norm, conf_dw, conf_dw_b, conf_ln_g, conf_ln_b, rwkv_mu, rwkv_w0, rwkv_w2, rwkv_a0, rwkv_a2, rwkv_g2, rwkv_kk, rwkv_ka, rwkv_rk, rwkv_ln_g, rwkv_ln_b, sc_w, w_out, norm_ffn, peer_wq, peer_k1, peer_k2, peer_u, peer_v):
    depth = w_in.shape[0]
    bp, seq, _ = x_prompt.shape
    db, dec_seq, _ = x_sample.shape
    n_pool = cache_k.shape[1]
    past_len = page_table.shape[1] * PAGE
    ck = jnp.transpose(cache_k, (0, 1, 3, 4, 2))
    cv = jnp.transpose(cache_v, (0, 1, 3, 4, 2))
    cos_p, sin_p = _rope_tables(jnp.arange(seq))
    cos_s, sin_s = _rope_tables(past_len + jnp.arange(dec_seq))
    dt = x_prompt.dtype

    yp, ys = x_prompt, x_sample
    outs_p = [[] for _ in range(6)]
    outs_s = [[] for _ in range(6)]
    for l in range(depth):
        p = _layer_params(l, norm_mix, w_in, q_norm, k_norm, conf_dw, conf_dw_b, conf_ln_g, conf_ln_b,
                          rwkv_mu, rwkv_w0, rwkv_w2, rwkv_a0, rwkv_a2, rwkv_g2, rwkv_kk, rwkv_ka,
                          rwkv_rk, rwkv_ln_g, rwkv_ln_b, sc_w, w_out, norm_ffn, peer_wq, peer_k1,
                          peer_k2, peer_u, peer_v)
        att, ob, oc, od, conf_new, shift_new, wkv_new, sc_new = _common_mixers(
            yp, p,
            jnp.zeros((bp, CONF_TAPS - 1, GROUP), dt),
            jnp.zeros((bp, rwkv_mu.shape[1]), dt),
            jnp.zeros((bp, N_HEADS, HEAD_DIM, HEAD_DIM), dt),
            jnp.zeros((bp, SC_TAPS - 1, GROUP), dt),
            chunk=64, nseq=_seq_group(bp, 4), conv_tile=_row_tile(seq, 512))
        oa, k_rot = _moba_prompt(att, cos_p, sin_p, p["qg"], p["kg"])
        new_p = (k_rot.reshape(bp, seq, N_HEADS, HEAD_DIM),
                 att[:, :, 2 * GROUP:].reshape(bp, seq, N_HEADS, HEAD_DIM),
                 conf_new, shift_new, wkv_new, sc_new)
        yp = _mix_and_ffn(yp, oa, ob, oc, od, p)

        att, ob, oc, od, conf_new, shift_new, wkv_new, sc_new = _common_mixers(
            ys, p, state_conf_conv[l], state_rwkv_shift[l], state_rwkv_wkv[l], state_short_conv[l],
            chunk=dec_seq, nseq=_seq_group(db, 8), conv_tile=dec_seq)
        sel, q_rot, k_rot = _decode_sel(att, cos_s, sin_s, p["qg"], p["kg"], ck, page_table, l)
        v_new = att[:, :, 2 * GROUP:]
        oa = _decode_att(sel, q_rot, k_rot, v_new, ck, cv, page_table, l)
        new_s = (k_rot.reshape(db, dec_seq, N_HEADS, HEAD_DIM),
                 v_new.reshape(db, dec_seq, N_HEADS, HEAD_DIM),
                 conf_new, shift_new, wkv_new, sc_new)
        ys = _mix_and_ffn(ys, oa, ob, oc, od, p)

        for lst, a in zip(outs_p, new_p):
            lst.append(a)
        for lst, a in zip(outs_s, new_s):
            lst.append(a)
    pk, pv, pconf, pshift, pwkv, psc = [jnp.stack(a) for a in outs_p]
    sk, sv, sconf, sshift, swkv, ssc = [jnp.stack(a) for a in outs_s]
    return (yp, ys, pk, pv, sk, sv, pconf, sconf, pshift, sshift, pwkv, swkv, psc, ssc)
```

```python
import functools
import math

import jax
import jax.numpy as jnp
from jax import lax
from jax.experimental import pallas as pl
from jax.experimental.pallas import tpu as pltpu

F32 = jnp.float32
BF16 = jnp.bfloat16

HEAD_DIM = 64
GROUP = 256
N_HEADS = GROUP // HEAD_DIM
MOBA_BLOCK = 256
MOBA_TOPK = 3
ROPE_THETA = 10000.0
PAGE = 128
CONF_TAPS = 31
SC_TAPS = 3
RWKV_LOW = 128
PEER_HEADS = 8
PEER_NKEYS = 128
PEER_TOPK = 16
PEER_EXPERT_BLOCK = 1024
DECODE_PAGES_PER_STEP = 16
RMS_EPS = 1e-6
LN_EPS = 1e-5
RWKV_GN_EPS = 64e-5
NEG_INF = -1e30
LANES = 128
VMEM_LIMIT = 56 * 1024 * 1024

_ARB = "arbitrary"


def _cparams(n_axes, vmem=VMEM_LIMIT):
    return pltpu.CompilerParams(dimension_semantics=(_ARB,) * n_axes, vmem_limit_bytes=vmem)


def _mm(a, b):
    return jnp.dot(a.astype(BF16), b.astype(BF16), preferred_element_type=F32)


def _mm_nt(a, b):
    return lax.dot_general(a.astype(BF16), b.astype(BF16), (((1,), (1,)), ((), ())),
                           preferred_element_type=F32)


def _split(a):
    hi = a.astype(BF16)
    lo = (a - hi.astype(F32)).astype(BF16)
    return hi, lo


def _mm3(a, b):
    ah, al = _split(a)
    bh, bl = _split(b)
    d = functools.partial(jnp.dot, preferred_element_type=F32)
    return d(ah, bh) + d(ah, bl) + d(al, bh)


def _mm3_nt(a, b):
    ah, al = _split(a)
    bh, bl = _split(b)
    d = functools.partial(lax.dot_general, dimension_numbers=(((1,), (1,)), ((), ())),
                          preferred_element_type=F32)
    return d(ah, bh) + d(ah, bl) + d(al, bh)


def _mm2_exact_rhs(a, b_bf16):
    ah, al = _split(a)
    d = functools.partial(jnp.dot, preferred_element_type=F32)
    return d(ah, b_bf16) + d(al, b_bf16)


def _sigmoid(x):
    return 1.0 / (1.0 + jnp.exp(-x))


def _head_of_lane(shape, axis):
    return lax.broadcasted_iota(jnp.int32, shape, axis) // HEAD_DIM


def _head_sum_matrix():
    r = _head_of_lane((GROUP, GROUP), 0)
    c = _head_of_lane((GROUP, GROUP), 1)
    return jnp.where(r == c, 1.0, 0.0).astype(BF16)


def _proj_kernel(x_ref, g_ref, wa_ref, wc_ref, wr_ref, ws_ref, oa_ref, oc_ref, or_ref, os_ref):
    x = x_ref[...]
    ms = jnp.mean(x * x, axis=-1, keepdims=True)
    h = (x * lax.rsqrt(ms + RMS_EPS) * g_ref[...]).astype(BF16)
    for w_ref, o_ref in ((wa_ref, oa_ref), (wc_ref, oc_ref), (wr_ref, or_ref), (ws_ref, os_ref)):
        o_ref[...] = jnp.dot(h, w_ref[...], preferred_element_type=F32)


def _proj(x, g, ws, tm):
    n, d = x.shape
    widths = [w.shape[1] for w in ws]
    return pl.pallas_call(
        _proj_kernel,
        grid=(n // tm,),
        in_specs=[pl.BlockSpec((tm, d), lambda i: (i, 0)),
                  pl.BlockSpec((1, d), lambda i: (0, 0))]
                 + [pl.BlockSpec((d, wd), lambda i: (0, 0)) for wd in widths],
        out_specs=[pl.BlockSpec((tm, wd), lambda i: (i, 0)) for wd in widths],
        out_shape=[jax.ShapeDtypeStruct((n, wd), F32) for wd in widths],
        compiler_params=_cparams(1),
        name="proj",
    )(x, g, *ws)


def _head_rms(x, gain, bd):
    ms = _mm2_exact_rhs(x * x, bd) * (1.0 / HEAD_DIM)
    return x * lax.rsqrt(ms + RMS_EPS) * gain


def _rope(x, cos, sin_signed):
    lane = lax.broadcasted_iota(jnp.int32, x.shape, 1) % HEAD_DIM
    lo = pltpu.roll(x, GROUP - HEAD_DIM // 2, axis=1)
    hi = pltpu.roll(x, HEAD_DIM // 2, axis=1)
    rot = jnp.where(lane < HEAD_DIM // 2, lo, hi)
    return x * cos + rot * sin_signed


def _top_rows(g, n_rows, k):
    ridx = lax.broadcasted_iota(jnp.int32, g.shape, 0).astype(F32)
    sel = jnp.zeros(g.shape, F32)
    cur = g
    for _ in range(k):
        mx = jnp.max(cur, axis=0, keepdims=True)
        first = jnp.min(jnp.where(cur == mx, ridx, float(n_rows)), axis=0, keepdims=True)
        hit = ridx == first
        sel = jnp.where(hit, 1.0, sel)
        cur = jnp.where(hit, -jnp.inf, cur)
    return sel


def _moba_prompt_kernel(att_ref, cos_ref, sin_ref, qg_ref, kg_ref, oa_ref, kr_ref,
                        kb_ref, vt_ref, km_ref, sel_ref, qm_ref):
    qi = pl.program_id(1)
    nblk = kb_ref.shape[0]
    blk = MOBA_BLOCK
    bd = _head_sum_matrix()

    @pl.when(qi == 0)
    def _():
        km_ref[...] = jnp.zeros(km_ref.shape, F32)

    a = att_ref[0]
    cos = cos_ref[...]
    sin = sin_ref[...]
    qr = _rope(_head_rms(a[:, 0:GROUP], qg_ref[...], bd), cos, sin)
    kr = _rope(_head_rms(a[:, GROUP:2 * GROUP], kg_ref[...], bd), cos, sin)
    v = a[:, 2 * GROUP:3 * GROUP]
    kr_ref[0] = kr
    kb_ref[qi] = kr.astype(BF16)
    vt_ref[qi] = v.T.astype(BF16)
    km_ref[pl.ds(qi, 1), :] = jnp.mean(kr, axis=0, keepdims=True)

    qs = qr * (HEAD_DIM ** -0.5)
    lane_head = _head_of_lane((blk, GROUP), 1)
    km = km_ref[...]
    km_head = _head_of_lane(km.shape, 1)
    blk_row = lax.broadcasted_iota(jnp.int32, (nblk, blk), 0)
    krow = lax.broadcasted_iota(jnp.int32, (blk, blk), 0)
    qcol = lax.broadcasted_iota(jnp.int32, (blk, blk), 1)
    kcur = kb_ref[qi]
    heads = range(N_HEADS)
    head_rows = [slice(h * HEAD_DIM, (h + 1) * HEAD_DIM) for h in heads]
    nt = functools.partial(lax.dot_general, dimension_numbers=(((1,), (1,)), ((), ())),
                           preferred_element_type=F32)
    qms = [jnp.where(lane_head == h, qs, 0.0) for h in heads]
    for h in heads:
        qm_ref[h] = qms[h].astype(BF16)
    past = blk_row < qi
    gates = [_mm3_nt(jnp.where(km_head == h, km, 0.0), qms[h]) for h in heads]
    for h in heads:
        sel = _top_rows(jnp.where(past, gates[h], NEG_INF), nblk, MOBA_TOPK)
        sel_ref[h] = jnp.where(past, sel, 0.0)
    sts = [jnp.where(krow <= qcol, nt(kcur, qm_ref[h]), NEG_INF) for h in heads]
    ms = [jnp.max(st, axis=0, keepdims=True) for st in sts]
    ps = [jnp.exp(st - m) for st, m in zip(sts, ms)]
    ls = [jnp.sum(p, axis=0, keepdims=True) for p in ps]
    accs = [jnp.dot(vt_ref[qi, head_rows[h], :], ps[h].astype(BF16), preferred_element_type=F32)
            for h in heads]

    def past_block(j, carry):
        ms, ls, accs = carry
        kj = kb_ref[j]
        sts = [jnp.where(sel_ref[h, pl.ds(j, 1), :] > 0.0, nt(kj, qm_ref[h]), NEG_INF) for h in heads]
        m_new = [jnp.maximum(m, jnp.max(st, axis=0, keepdims=True)) for m, st in zip(ms, sts)]
        alpha = [jnp.exp(m - mn) for m, mn in zip(ms, m_new)]
        ps = [jnp.exp(st - mn) for st, mn in zip(sts, m_new)]
        ls = [a * l + jnp.sum(p, axis=0, keepdims=True) for a, l, p in zip(alpha, ls, ps)]
        accs = [alpha[h] * accs[h] + jnp.dot(vt_ref[j, head_rows[h], :], ps[h].astype(BF16),
                                             preferred_element_type=F32) for h in heads]
        return m_new, ls, accs

    ms, ls, accs = lax.fori_loop(0, qi, past_block, (ms, ls, accs))
    oa_ref[0] = jnp.concatenate([acc / l for acc, l in zip(accs, ls)], axis=0).T


def _moba_prompt(att, cos, sin, qg, kg):
    b, t, _ = att.shape
    nblk = t // MOBA_BLOCK
    blk = MOBA_BLOCK
    return pl.pallas_call(
        _moba_prompt_kernel,
        grid=(b, nblk),
        in_specs=[pl.BlockSpec((1, blk, 3 * GROUP), lambda i, j: (i, j, 0)),
                  pl.BlockSpec((blk, GROUP), lambda i, j: (j, 0)),
                  pl.BlockSpec((blk, GROUP), lambda i, j: (j, 0)),
                  pl.BlockSpec((1, GROUP), lambda i, j: (0, 0)),
                  pl.BlockSpec((1, GROUP), lambda i, j: (0, 0))],
        out_specs=[pl.BlockSpec((1, blk, GROUP), lambda i, j: (i, j, 0)),
                   pl.BlockSpec((1, blk, GROUP), lambda i, j: (i, j, 0))],
        out_shape=[jax.ShapeDtypeStruct((b, t, GROUP), F32),
                   jax.ShapeDtypeStruct((b, t, GROUP), F32)],
        scratch_shapes=[pltpu.VMEM((nblk, blk, GROUP), BF16),
                        pltpu.VMEM((nblk, GROUP, blk), BF16),
                        pltpu.VMEM((nblk, GROUP), F32),
                        pltpu.VMEM((N_HEADS, nblk, blk), F32),
                        pltpu.VMEM((N_HEADS, blk, GROUP), BF16)],
        compiler_params=_cparams(2),
        name="moba_prompt",
    )(att, cos, sin, qg, kg)


def _stack_heads(x):
    lane_head = _head_of_lane(x.shape, 1)
    return jnp.concatenate([jnp.where(lane_head == h, x, 0.0) for h in range(N_HEADS)], axis=0)


def _decode_sel_kernel(pt_ref, att_ref, cos_ref, sin_ref, qg_ref, kg_ref, *rest, pages_per_step):
    page_refs = rest[:pages_per_step]
    sel_ref, qr_ref, kr_ref, q2_ref, gate_ref = rest[pages_per_step:]
    s = pl.program_id(1)
    n_steps = pl.num_programs(1)
    blocks_per_step = pages_per_step * PAGE // MOBA_BLOCK
    n_past = n_steps * blocks_per_step
    pages_per_block = MOBA_BLOCK // PAGE
    rows = gate_ref.shape[0]
    bd = _head_sum_matrix()

    @pl.when(s == 0)
    def _():
        gate_ref[...] = jnp.zeros(gate_ref.shape, F32)
        a = att_ref[0]
        qr = _rope(_head_rms(a[:, 0:GROUP], qg_ref[...], bd), cos_ref[...], sin_ref[...])
        qr_ref[0] = qr
        kr_ref[0] = _rope(_head_rms(a[:, GROUP:2 * GROUP], kg_ref[...], bd), cos_ref[...], sin_ref[...])
        hi, lo = _split(_stack_heads(qr * (HEAD_DIM ** -0.5)))
        q2_ref[...] = jnp.concatenate([hi, lo], axis=0)

    lane = lax.broadcasted_iota(jnp.int32, gate_ref.shape, 1)
    for bi in range(blocks_per_step):
        tot = jnp.zeros((rows, 1), F32)
        for pi in range(pages_per_block):
            kt = page_refs[bi * pages_per_block + pi][0, 0].reshape(GROUP, PAGE).astype(BF16)
            sc = jnp.dot(q2_ref[...], kt, preferred_element_type=F32)
            tot = tot + jnp.sum(sc[0:rows, :] + sc[rows:2 * rows, :], axis=1, keepdims=True)
        gate_ref[...] += jnp.where(lane == s * blocks_per_step + bi, tot * (1.0 / MOBA_BLOCK), 0.0)

    @pl.when(s == n_steps - 1)
    def _():
        gate = gate_ref[...]
        past = lane < n_past
        cur = jnp.where(past, gate, NEG_INF)
        lidx = lane.astype(F32)
        sel = jnp.zeros(gate.shape, F32)
        for _ in range(MOBA_TOPK):
            mx = jnp.max(cur, axis=1, keepdims=True)
            first = jnp.min(jnp.where(cur == mx, lidx, float(LANES)), axis=1, keepdims=True)
            hit = lidx == first
            sel = jnp.where(hit, 1.0, sel)
            cur = jnp.where(hit, -jnp.inf, cur)
        sel_ref[0] = jnp.where(past, sel, 0.0)


def _decode_sel(att, cos, sin, qg, kg, cache_kt, page_table, layer):
    db, t, _ = att.shape
    n_pages = page_table.shape[1]
    pages_per_step = math.gcd(n_pages, DECODE_PAGES_PER_STEP)
    n_steps = n_pages // pages_per_step
    rows = N_HEADS * t
    assert n_pages * PAGE // MOBA_BLOCK <= LANES

    def page_map(i):
        return lambda b, s, pt: (layer, pt[b, s * pages_per_step + i], 0, 0, 0)

    grid_spec = pltpu.PrefetchScalarGridSpec(
        num_scalar_prefetch=1,
        grid=(db, n_steps),
        in_specs=[pl.BlockSpec((1, t, 3 * GROUP), lambda b, s, pt: (b, 0, 0)),
                  pl.BlockSpec((t, GROUP), lambda b, s, pt: (0, 0)),
                  pl.BlockSpec((t, GROUP), lambda b, s, pt: (0, 0)),
                  pl.BlockSpec((1, GROUP), lambda b, s, pt: (0, 0)),
                  pl.BlockSpec((1, GROUP), lambda b, s, pt: (0, 0))]
                 + [pl.BlockSpec((1, 1, N_HEADS, HEAD_DIM, PAGE), page_map(i)) for i in range(pages_per_step)],
        out_specs=[pl.BlockSpec((1, rows, LANES), lambda b, s, pt: (b, 0, 0)),
                   pl.BlockSpec((1, t, GROUP), lambda b, s, pt: (b, 0, 0)),
                   pl.BlockSpec((1, t, GROUP), lambda b, s, pt: (b, 0, 0))],
        scratch_shapes=[pltpu.VMEM((2 * rows, GROUP), BF16),
                        pltpu.VMEM((rows, LANES), F32)],
    )
    return pl.pallas_call(
        functools.partial(_decode_sel_kernel, pages_per_step=pages_per_step),
        grid_spec=grid_spec,
        out_shape=[jax.ShapeDtypeStruct((db, N_HEADS * t, LANES), F32),
                   jax.ShapeDtypeStruct((db, t, GROUP), F32),
                   jax.ShapeDtypeStruct((db, t, GROUP), F32)],
        compiler_params=_cparams(2),
        name="decode_sel",
    )(page_table, att, cos, sin, qg, kg, *([cache_kt] * pages_per_step))


def _decode_att_kernel(pt_ref, sel_ref, qr_ref, kn_ref, vn_ref, *rest, pages_per_step):
    k_refs = rest[:pages_per_step]
    v_refs = rest[pages_per_step:2 * pages_per_step]
    oa_ref, qbd_ref, acc_ref, m_ref, l_ref = rest[2 * pages_per_step:]
    j = pl.program_id(1)
    n_steps = pl.num_programs(1)
    t = qr_ref.shape[1]
    blocks_per_step = pages_per_step * PAGE // MOBA_BLOCK

    @pl.when(j == 0)
    def _():
        qbd_ref[...] = _stack_heads(qr_ref[0] * (HEAD_DIM ** -0.5)).astype(BF16)
        pad = jnp.zeros((MOBA_BLOCK - t, GROUP), F32)
        kn = jnp.concatenate([kn_ref[0], pad], axis=0).astype(BF16)
        vn = jnp.concatenate([vn_ref[0], pad], axis=0).astype(BF16)
        st = lax.dot_general(qbd_ref[...], kn, (((1,), (1,)), ((), ())), preferred_element_type=F32)
        qpos = lax.broadcasted_iota(jnp.int32, st.shape, 0) % t
        kpos = lax.broadcasted_iota(jnp.int32, st.shape, 1)
        st = jnp.where(kpos <= qpos, st, NEG_INF)
        m = jnp.max(st, axis=1, keepdims=True)
        p = jnp.exp(st - m)
        m_ref[...] = m
        l_ref[...] = jnp.sum(p, axis=1, keepdims=True)
        acc_ref[...] = jnp.dot(p.astype(BF16), vn, preferred_element_type=F32)

    qbd = qbd_ref[...]
    pages_per_block = MOBA_BLOCK // PAGE
    sel = sel_ref[0]
    lane = lax.broadcasted_iota(jnp.int32, sel.shape, 1)
    masked = []
    for pi, r in enumerate(k_refs):
        sc = jnp.dot(qbd, r[0, 0].reshape(GROUP, PAGE).astype(BF16), preferred_element_type=F32)
        blk = j * blocks_per_step + pi // pages_per_block
        chosen = jnp.sum(jnp.where(lane == blk, sel, 0.0), axis=1, keepdims=True) > 0.0
        masked.append(jnp.where(chosen, sc, NEG_INF))
    st = jnp.concatenate(masked, axis=1)
    m_old = m_ref[...]
    m_new = jnp.maximum(m_old, jnp.max(st, axis=1, keepdims=True))
    alpha = jnp.exp(m_old - m_new)
    p32 = jnp.exp(st - m_new)
    p = p32.astype(BF16)
    m_ref[...] = m_new
    l_ref[...] = alpha * l_ref[...] + jnp.sum(p32, axis=1, keepdims=True)
    acc = alpha * acc_ref[...]
    for pi, r in enumerate(v_refs):
        acc = acc + lax.dot_general(p[:, pi * PAGE:(pi + 1) * PAGE], r[0, 0].reshape(GROUP, PAGE).astype(BF16),
                                    (((1,), (1,)), ((), ())), preferred_element_type=F32)
    acc_ref[...] = acc

    @pl.when(j == n_steps - 1)
    def _():
        o = acc_ref[...] / l_ref[...]
        lane_head = _head_of_lane((t, GROUP), 1)
        out = jnp.zeros((t, GROUP), F32)
        for h in range(N_HEADS):
            out = out + jnp.where(lane_head == h, o[h * t:(h + 1) * t, :], 0.0)
        oa_ref[0] = out


def _decode_att(sel, qr, kn, vn, cache_kt, cache_vt, page_table, layer):
    db, t, _ = qr.shape
    pages_per_step = math.gcd(page_table.shape[1], DECODE_PAGES_PER_STEP)
    n_steps = page_table.shape[1] // pages_per_step
    rows = N_HEADS * t

    def page_map(i):
        return lambda b, j, pt: (layer, pt[b, pages_per_step * j + i], 0, 0, 0)

    pages = [pl.BlockSpec((1, 1, N_HEADS, HEAD_DIM, PAGE), page_map(i)) for i in range(pages_per_step)]
    grid_spec = pltpu.PrefetchScalarGridSpec(
        num_scalar_prefetch=1,
        grid=(db, n_steps),
        in_specs=[pl.BlockSpec((1, rows, LANES), lambda b, j, pt: (b, 0, 0)),
                  pl.BlockSpec((1, t, GROUP), lambda b, j, pt: (b, 0, 0)),
                  pl.BlockSpec((1, t, GROUP), lambda b, j, pt: (b, 0, 0)),
                  pl.BlockSpec((1, t, GROUP), lambda b, j, pt: (b, 0, 0))] + pages + pages,
        out_specs=[pl.BlockSpec((1, t, GROUP), lambda b, j, pt: (b, 0, 0))],
        scratch_shapes=[pltpu.VMEM((rows, GROUP), BF16),
                        pltpu.VMEM((rows, GROUP), F32),
                        pltpu.VMEM((rows, 1), F32),
                        pltpu.VMEM((rows, 1), F32)],
    )
    return pl.pallas_call(
        functools.partial(_decode_att_kernel, pages_per_step=pages_per_step),
        grid_spec=grid_spec,
        out_shape=[jax.ShapeDtypeStruct((db, t, GROUP), F32)],
        compiler_params=_cparams(2),
        name="decode_att",
    )(page_table, sel, qr, kn, vn, *([cache_kt] * pages_per_step), *([cache_vt] * pages_per_step))[0]


def _conv_kernel(conf_ref, sc_ref, cbuf_ref, sbuf_ref, dw_ref, dwb_ref, lng_ref, lnb_ref, scw_ref,
                 ob_ref, od_ref, cnew_ref, snew_ref, extc_ref, exts_ref):
    ti = pl.program_id(1)
    nt = pl.num_programs(1)
    tt = conf_ref.shape[1]
    hc = 32
    hs = 8
    nbc = CONF_TAPS - 1
    nbs = SC_TAPS - 1

    @pl.when(ti == 0)
    def _():
        extc_ref[0:hc - nbc, :] = jnp.zeros((hc - nbc, GROUP), F32)
        extc_ref[hc - nbc:hc, :] = cbuf_ref[0]
        exts_ref[0:hs - nbs, :] = jnp.zeros((hs - nbs, GROUP), F32)
        exts_ref[hs - nbs:hs, :] = sbuf_ref[0]

    c = conf_ref[0]
    glu = c[:, 0:GROUP] * _sigmoid(c[:, GROUP:2 * GROUP])
    extc_ref[hc:hc + tt, :] = glu
    acc = jnp.zeros((tt, GROUP), F32) + dwb_ref[...]
    for j in range(CONF_TAPS):
        acc = acc + dw_ref[j:j + 1, :] * extc_ref[pl.ds(hc - nbc + j, tt), :]
    mu = jnp.mean(acc, axis=-1, keepdims=True)
    var = jnp.mean(jnp.square(acc - mu), axis=-1, keepdims=True)
    ln = (acc - mu) * lax.rsqrt(var + LN_EPS) * lng_ref[...] + lnb_ref[...]
    ob_ref[0] = ln * _sigmoid(ln)

    s = sc_ref[0]
    gated = s[:, GROUP:2 * GROUP] * s[:, 2 * GROUP:3 * GROUP]
    exts_ref[hs:hs + tt, :] = gated
    d = jnp.zeros((tt, GROUP), F32)
    for j in range(SC_TAPS):
        d = d + scw_ref[j:j + 1, :] * exts_ref[pl.ds(hs - nbs + j, tt), :]
    od_ref[0] = s[:, 0:GROUP] * d

    @pl.when(ti == nt - 1)
    def _():
        cnew_ref[0] = extc_ref[pl.ds(tt + hc - nbc, nbc), :]
        snew_ref[0] = exts_ref[pl.ds(tt + hs - nbs, nbs), :]

    tail_c = extc_ref[pl.ds(tt, hc), :]
    tail_s = exts_ref[pl.ds(tt, hs), :]
    extc_ref[0:hc, :] = tail_c
    exts_ref[0:hs, :] = tail_s


def _conv(conf, sc, cbuf, sbuf, dw, dwb, lng, lnb, scw, tt):
    b, t, _ = conf.shape
    nbc = CONF_TAPS - 1
    nbs = SC_TAPS - 1
    const = lambda i, j: (0, 0)
    return pl.pallas_call(
        _conv_kernel,
        grid=(b, t // tt),
        in_specs=[pl.BlockSpec((1, tt, 2 * GROUP), lambda i, j: (i, j, 0)),
                  pl.BlockSpec((1, tt, 3 * GROUP), lambda i, j: (i, j, 0)),
                  pl.BlockSpec((1, nbc, GROUP), lambda i, j: (i, 0, 0)),
                  pl.BlockSpec((1, nbs, GROUP), lambda i, j: (i, 0, 0)),
                  pl.BlockSpec((CONF_TAPS, GROUP), const),
                  pl.BlockSpec((1, GROUP), const),
                  pl.BlockSpec((1, GROUP), const),
                  pl.BlockSpec((1, GROUP), const),
                  pl.BlockSpec((SC_TAPS, GROUP), const)],
        out_specs=[pl.BlockSpec((1, tt, GROUP), lambda i, j: (i, j, 0)),
                   pl.BlockSpec((1, tt, GROUP), lambda i, j: (i, j, 0)),
                   pl.BlockSpec((1, nbc, GROUP), lambda i, j: (i, 0, 0)),
                   pl.BlockSpec((1, nbs, GROUP), lambda i, j: (i, 0, 0))],
        out_shape=[jax.ShapeDtypeStruct((b, t, GROUP), F32),
                   jax.ShapeDtypeStruct((b, t, GROUP), F32),
                   jax.ShapeDtypeStruct((b, nbc, GROUP), F32),
                   jax.ShapeDtypeStruct((b, nbs, GROUP), F32)],
        scratch_shapes=[pltpu.VMEM((32 + tt, GROUP), F32),
                        pltpu.VMEM((8 + tt, GROUP), F32)],
        compiler_params=_cparams(2),
        name="conv",
    )(conf, sc, cbuf, sbuf, dw, dwb, lng, lnb, scw)


def _rwkv_kernel(rw_ref, shift_ref, st0_ref, mu_ref, w0_ref, a0_ref, kkw_ref, ka_ref, rk_ref,
                 lng_ref, lnb_ref, w2_ref, a2_ref, g2_ref, oc_ref, sto_ref, st_ref, prev_ref):
    c = pl.program_id(1)

    @pl.when(c == 0)
    def _():
        st_ref[...] = st0_ref[...]
        prev_ref[...] = shift_ref[...]

    nseq, L, _ = rw_ref.shape
    SL = max(N_HEADS * L, LANES)
    seqs = range(nseq)
    each = lambda f, *cols: [f(*args) for args in zip(*cols)]
    P = [_rwkv_prep(s, rw_ref, mu_ref, w0_ref, a0_ref, kkw_ref, ka_ref, w2_ref, a2_ref, g2_ref, prev_ref)
         for s in seqs]
    at_s, bt_s, kt_s, rt_s, bg_s, kg_s, v_s = ([p[i] for p in P] for i in range(7))

    srow = lax.broadcasted_iota(jnp.int32, (SL, SL), 0)
    scol = lax.broadcasted_iota(jnp.int32, (SL, SL), 1)
    same = (srow // L) == (scol // L)
    strict = same & ((srow % L) > (scol % L))
    incl = same & ((srow % L) >= (scol % L))
    eye = jnp.where(srow == scol, 1.0, 0.0)
    n_ab = each(lambda a, b: jnp.where(strict, _mm_nt(a, b), 0.0), at_s, bt_s)
    n_ak = each(lambda a, b: jnp.where(strict, _mm_nt(a, b), 0.0), at_s, kt_s)
    q_rb = each(lambda a, b: jnp.where(incl, _mm_nt(a, b), 0.0), rt_s, bt_s)
    q_rk = each(lambda a, b: jnp.where(incl, _mm_nt(a, b), 0.0), rt_s, kt_s)

    tm = [eye + n for n in n_ab]
    pw = n_ab
    for _ in range(int(math.log2(L)) - 1):
        pw = each(_mm, pw, pw)
        tm = each(lambda t, x: t + _mm(t, x), tm, pw)

    p1 = each(_mm, tm, at_s)
    p2 = each(_mm, tm, each(_mm, n_ak, v_s))
    bg_t = [u.T for u in bg_s]
    kg_t = [u.T for u in kg_s]
    drow = lax.broadcasted_iota(jnp.int32, (GROUP, GROUP), 0)
    dcol = lax.broadcasted_iota(jnp.int32, (GROUP, GROUP), 1)
    m_mat = each(lambda p, bt, x: jnp.where(drow == dcol, jnp.exp(p[7]), 0.0) + _mm3(bt, x), P, bg_t, p1)
    c_mat = each(lambda bt, x, kt, vs: _mm3(bt, x) + _mm3(kt, vs), bg_t, p2, kg_t, v_s)
    yc = each(lambda rt, q, x: rt + _mm(q, x), rt_s, q_rb, p1)
    y0 = each(lambda q, x, qk, vs: _mm(q, x) + _mm(qk, vs), q_rb, p2, q_rk, v_s)

    st = [st_ref[s] for s in seqs]
    y_s = each(lambda a, b, y: _mm3(a, b) + y, yc, st, y0)
    st_new = each(lambda m, b, cm: _mm3(m, b) + cm, m_mat, st, c_mat)
    bd = _head_sum_matrix()
    inv_n = 1.0 / HEAD_DIM
    for s in seqs:
        st_ref[s] = st_new[s]
        y = y_s[s][0:L, :]
        for h in range(1, N_HEADS):
            y = y + y_s[s][h * L:(h + 1) * L, :]
        r, k2, v, g = P[s][8:12]
        mean = _mm2_exact_rhs(y, bd) * inv_n
        yc0 = y - mean
        var = _mm2_exact_rhs(yc0 * yc0, bd) * inv_n
        yn = yc0 * lax.rsqrt(var + RWKV_GN_EPS) * lng_ref[...] + lnb_ref[...]
        bonus = _mm2_exact_rhs(r * k2 * rk_ref[...], bd) * v
        oc_ref[s] = (yn + bonus) * g

    @pl.when(c == pl.num_programs(1) - 1)
    def _():
        sto_ref[...] = st_ref[...]


def _rwkv_prep(s, rw_ref, mu_ref, w0_ref, a0_ref, kkw_ref, ka_ref, w2_ref, a2_ref, g2_ref, prev_ref):
    L = rw_ref.shape[1]
    SL = max(N_HEADS * L, LANES)
    bd = _head_sum_matrix()

    z = rw_ref[s]
    row = lax.broadcasted_iota(jnp.int32, z.shape, 0)
    zprev = jnp.where(row == 0, prev_ref[s], pltpu.roll(z, 1, axis=0))
    prev_ref[s] = z[L - 1:L, :]
    zs = z + (zprev - z) * mu_ref[...]
    r = zs[:, 0:GROUP]
    k = zs[:, GROUP:2 * GROUP]
    v = zs[:, 2 * GROUP:3 * GROUP]
    low = zs[:, 3 * GROUP:3 * GROUP + RWKV_LOW]

    x = -(w0_ref[...] + _mm3(jnp.tanh(low), w2_ref[...]))
    softplus = jnp.maximum(x, 0.0) + jnp.log(1.0 + jnp.exp(-jnp.abs(x)))
    ld = -jnp.exp(-softplus - 0.5)
    a = _sigmoid(a0_ref[...] + _mm3(low, a2_ref[...]))
    g = _mm(_sigmoid(low), g2_ref[...])
    kk = k * kkw_ref[...]
    kk = kk / jnp.maximum(jnp.sqrt(_mm2_exact_rhs(kk * kk, bd)), 1e-12)
    k2 = k * (1.0 + (a - 1.0) * ka_ref[...])

    trow = lax.broadcasted_iota(jnp.int32, (L, L), 0)
    tcol = lax.broadcasted_iota(jnp.int32, (L, L), 1)
    tri = jnp.where(trow >= tcol, 1.0, 0.0).astype(BF16)
    cum = _mm2_exact_rhs_lhs(tri, ld)
    cend = cum[L - 1:L, :]
    g_prev = jnp.exp(cum - ld)
    g_inv = jnp.exp(-cum)
    g_rel = jnp.exp(cend - cum)
    kka = kk * a

    lane_head = _head_of_lane((L, GROUP), 1)

    def stack(u):
        parts = [jnp.where(lane_head == h, u, 0.0) for h in range(N_HEADS)]
        if SL > N_HEADS * L:
            parts.append(jnp.zeros((SL - N_HEADS * L, GROUP), F32))
        return jnp.concatenate(parts, axis=0)

    at_s = stack(-kk * g_prev)
    bt_s = stack(kka * g_inv)
    kt_s = stack(k2 * g_inv)
    rt_s = stack(r * jnp.exp(cum))
    bg_s = stack(kka * g_rel)
    kg_s = stack(k2 * g_rel)
    v_s = stack(v)
    return at_s, bt_s, kt_s, rt_s, bg_s, kg_s, v_s, cend, r, k2, v, g


def _mm2_exact_rhs_lhs(a_bf16, b):
    bh, bl = _split(b)
    d = functools.partial(jnp.dot, preferred_element_type=F32)
    return d(a_bf16, bh) + d(a_bf16, bl)


def _rwkv(rw, shift, st0, p, L, nseq=2):
    b, t, w = rw.shape
    const = lambda i, j: (0, 0)
    vec = pl.BlockSpec((1, GROUP), const)
    low = pl.BlockSpec((RWKV_LOW, GROUP), const)
    return pl.pallas_call(
        _rwkv_kernel,
        grid=(b // nseq, t // L),
        in_specs=[pl.BlockSpec((nseq, L, w), lambda i, j: (i, j, 0)),
                  pl.BlockSpec((nseq, 1, w), lambda i, j: (i, 0, 0)),
                  pl.BlockSpec((nseq, GROUP, GROUP), lambda i, j: (i, 0, 0)),
                  pl.BlockSpec((1, w), const),
                  vec, vec, vec, vec, vec, vec, vec, low, low, low],
        out_specs=[pl.BlockSpec((nseq, L, GROUP), lambda i, j: (i, j, 0)),
                   pl.BlockSpec((nseq, GROUP, GROUP), lambda i, j: (i, 0, 0))],
        out_shape=[jax.ShapeDtypeStruct((b, t, GROUP), F32),
                   jax.ShapeDtypeStruct((b, GROUP, GROUP), F32)],
        scratch_shapes=[pltpu.VMEM((nseq, GROUP, GROUP), F32),
                        pltpu.VMEM((nseq, 1, w), F32)],
        compiler_params=_cparams(2),
        name="rwkv",
    )(rw, shift, st0, p["mu"], p["w0"], p["a0"], p["kkw"], p["ka"], p["rk"], p["lng"], p["lnb"],
      p["w2"], p["a2"], p["g2"])


def _state_to_blockdiag(s):
    b = s.shape[0]
    st = jnp.swapaxes(s, -1, -2)
    eye = jnp.eye(N_HEADS, dtype=s.dtype)
    return jnp.einsum("bhkv,hg->bhkgv", st, eye).reshape(b, GROUP, GROUP)


def _blockdiag_to_state(st):
    b = st.shape[0]
    x = st.reshape(b, N_HEADS, HEAD_DIM, N_HEADS, HEAD_DIM)
    diag = jnp.stack([x[:, h, :, h, :] for h in range(N_HEADS)], axis=1)
    return jnp.swapaxes(diag, -1, -2)


def _outproj_kernel(oa_ref, ob_ref, oc_ref, od_ref, x_ref, wo_ref, g_ref, wq_ref, x1_ref, h_ref, q_ref):
    acc = x_ref[...]
    for i, o_ref in enumerate((oa_ref, ob_ref, oc_ref, od_ref)):
        acc = acc + jnp.dot(o_ref[...].astype(BF16), wo_ref[i], preferred_element_type=F32)
    x1_ref[...] = acc
    ms = jnp.mean(acc * acc, axis=-1, keepdims=True)
    h = (acc * lax.rsqrt(ms + RMS_EPS) * g_ref[...]).astype(BF16)
    h_ref[...] = h
    q = jnp.dot(h, wq_ref[...], preferred_element_type=F32)
    qd = q_ref.shape[2]
    for hd in range(q_ref.shape[0]):
        q_ref[hd] = q[:, hd * qd:(hd + 1) * qd]


def _outproj(oa, ob, oc, od, x, wo, g, wq, tm):
    n, d = x.shape
    dq = wq.shape[1]
    mix = pl.BlockSpec((tm, GROUP), lambda i: (i, 0))
    return pl.pallas_call(
        _outproj_kernel,
        grid=(n // tm,),
        in_specs=[mix, mix, mix, mix,
                  pl.BlockSpec((tm, d), lambda i: (i, 0)),
                  pl.BlockSpec((4, GROUP, d), lambda i: (0, 0, 0)),
                  pl.BlockSpec((1, d), lambda i: (0, 0)),
                  pl.BlockSpec((d, dq), lambda i: (0, 0))],
        out_specs=[pl.BlockSpec((tm, d), lambda i: (i, 0)),
                   pl.BlockSpec((tm, d), lambda i: (i, 0)),
                   pl.BlockSpec((PEER_HEADS, tm, dq // PEER_HEADS), lambda i: (0, i, 0))],
        out_shape=[jax.ShapeDtypeStruct((n, d), F32),
                   jax.ShapeDtypeStruct((n, d), BF16),
                   jax.ShapeDtypeStruct((PEER_HEADS, n, dq // PEER_HEADS), F32)],
        compiler_params=_cparams(1),
        name="outproj",
    )(oa, ob, oc, od, x, wo, g, wq)


_CAND_PIECES = ((0, 0, 8), (0, 8, 8), (1, 0, 8), (2, 0, 5), (3, 0, 4), (4, 0, 3), (5, 0, 2),
                (6, 0, 2), (7, 0, 2))
_CAND_ROWS = 8 * (len(_CAND_PIECES) + 1)


def _sort_network(n):
    size = 1
    while size < n:
        size *= 2
    pairs = []
    p = 1
    while p < size:
        k = p
        while k >= 1:
            for j in range(k % p, size - k, 2 * k):
                for i in range(min(k, size - j - k)):
                    if (i + j) // (2 * p) == (i + j + k) // (2 * p):
                        pairs.append((i + j, i + j + k))
            k //= 2
        p *= 2
    return tuple((i, j) for i, j in pairs if j < n)


_SUBLANES = 8


def _top_values(x, depth):
    groups = x.shape[0] // _SUBLANES
    v = [x[_SUBLANES * k:_SUBLANES * (k + 1), :] for k in range(groups)]
    for i, j in _sort_network(groups):
        v[i], v[j] = jnp.maximum(v[i], v[j]), jnp.minimum(v[i], v[j])
    sidx = lax.broadcasted_iota(jnp.int32, v[0].shape, 0).astype(F32)
    out = []
    for r in range(depth):
        mx = jnp.max(v[0], axis=0, keepdims=True)
        first = jnp.min(jnp.where(v[0] == mx, sidx, float(_SUBLANES)), axis=0, keepdims=True)
        pop = sidx == first
        out.append(mx)
        reachable = depth - 1 - r
        for k in range(min(groups - 1, reachable)):
            v[k] = jnp.where(pop, v[k + 1], v[k])
        if reachable > groups - 1:
            v[groups - 1] = jnp.where(pop, -jnp.inf, v[groups - 1])
    return jnp.concatenate(out, axis=0)


_SELECT_TILES_PER_ITER = 2


def _peer_select_kernel(q_ref, k1_ref, k2_ref, s1_ref, s2_ref, thr_ref, stat_ref):
    tn = q_ref.shape[1]
    n_lt = tn // LANES
    q = q_ref[0].astype(BF16)
    nt = functools.partial(lax.dot_general, dimension_numbers=(((1,), (1,)), ((), ())),
                           preferred_element_type=F32)
    s1_all = nt(k1_ref[...], q)
    s2_all = nt(k2_ref[...], q)
    for lt in range(n_lt):
        s1_ref[lt, 0] = s1_all[:, lt * LANES:(lt + 1) * LANES]
        s2_ref[lt, 0] = s2_all[:, lt * LANES:(lt + 1) * LANES]

    sub = lax.broadcasted_iota(jnp.int32, (8, LANES), 0)

    def one_tile(lt):
        s1 = s1_ref[lt, 0]
        v1 = _top_values(s1, PEER_TOPK)
        v2 = _top_values(s2_ref[lt, 0], PEER_TOPK)
        pieces = [jnp.where(sub < nb, v1[a:a + 1, :] + v2[b0:b0 + 8, :], -jnp.inf)
                  for a, b0, nb in _CAND_PIECES]
        pieces.append(v1[8:16, :] + v2[0:1, :])
        cv = _top_values(jnp.concatenate(pieces, axis=0), PEER_TOPK)
        tau = cv[PEER_TOPK - 1:PEER_TOPK, :]
        z = jnp.sum(jnp.exp(cv - cv[0:1, :]), axis=0, keepdims=True)
        stat_ref[lt, 0] = jnp.concatenate(
            [tau, v1[0:1, :], v2[0:1, :], 1.0 / z, jnp.zeros((4, LANES), F32)], axis=0)
        thr = jnp.full((PEER_NKEYS, LANES), jnp.inf, F32)
        for b in range(PEER_TOPK):
            vb = v2[b:b + 1, :]
            thr = jnp.where((s1 + vb) >= tau, vb, thr)
        thr_ref[lt, 0] = thr

    per_iter = math.gcd(n_lt, _SELECT_TILES_PER_ITER)

    def tile_body(i, carry):
        for u in range(per_iter):
            one_tile(i * per_iter + u)
        return carry

    lax.fori_loop(0, n_lt // per_iter, tile_body, 0)


def _peer_select(q, k1p, k2p, tn):
    _, n, qd = q.shape
    n_lt = tn // LANES
    keyed = pl.BlockSpec((n_lt, 1, PEER_NKEYS, LANES), lambda i, h: (i, h, 0, 0))
    keyed_shape = jax.ShapeDtypeStruct((n // LANES, PEER_HEADS, PEER_NKEYS, LANES), F32)
    return pl.pallas_call(
        _peer_select_kernel,
        grid=(n // tn, PEER_HEADS),
        in_specs=[pl.BlockSpec((1, tn, qd), lambda i, h: (h, i, 0)),
                  pl.BlockSpec((PEER_NKEYS, qd), lambda i, h: (0, 0)),
                  pl.BlockSpec((PEER_NKEYS, qd), lambda i, h: (0, 0))],
        out_specs=[keyed, keyed, keyed,
                   pl.BlockSpec((n_lt, 1, 8, LANES), lambda i, h: (i, h, 0, 0))],
        out_shape=[keyed_shape, keyed_shape, keyed_shape,
                   jax.ShapeDtypeStruct((n // LANES, PEER_HEADS, 8, LANES), F32)],
        compiler_params=_cparams(2),
        name="peer_select",
    )(q, k1p, k2p)


_PEER_SUB_KEYS = 2


def _peer_dense_kernel(h_ref, *rest, n_sub):
    u_refs = rest[:n_sub]
    vt_refs = rest[n_sub:2 * n_sub]
    s1_ref, s2_ref, thr_ref, stat_ref, x1_ref, y_ref, acc_ref, e1_ref, e2_ref = rest[2 * n_sub:]
    j = pl.program_id(1)
    nj = pl.num_programs(1)
    n_lt = s2_ref.shape[0]
    ne1 = n_sub * _PEER_SUB_KEYS

    @pl.when(j == 0)
    def _():
        acc_ref[...] = jnp.zeros(acc_ref.shape, F32)
        for lt in range(n_lt):
            for h in range(PEER_HEADS):
                st = stat_ref[lt, h]
                e1_ref[lt, h] = jnp.exp(s1_ref[lt, h] - st[1:2, :]) * st[3:4, :]
                e2_ref[lt, h] = jnp.exp(s2_ref[lt, h] - st[2:3, :])

    hx = h_ref[...]
    dep = jnp.zeros((1, LANES), F32)

    def activations(sb):
        return lax.dot_general(u_refs[sb][...], hx, (((1,), (1,)), ((), ())), preferred_element_type=F32)

    act_next = activations(0)
    for sb in range(n_sub):
        act = act_next
        if sb + 1 < n_sub:
            act_next = activations(sb + 1)
        w_rows = []
        for e in range(_PEER_SUB_KEYS):
            e1 = j * ne1 + sb * _PEER_SUB_KEYS + e
            w_tiles = []
            for lt in range(n_lt):
                gate = jnp.zeros((PEER_NKEYS, LANES), F32)
                for h in range(PEER_HEADS):
                    chosen = s2_ref[lt, h] >= (thr_ref[lt, h, pl.ds(e1, 1), :] + dep)
                    gate = gate + jnp.where(chosen, e2_ref[lt, h], 0.0) * e1_ref[lt, h, pl.ds(e1, 1), :]
                a = act[e * PEER_NKEYS:(e + 1) * PEER_NKEYS, lt * LANES:(lt + 1) * LANES]
                wt = gate * (0.5 * a * (1.0 + lax.erf(a * (2.0 ** -0.5))))
                bits = pltpu.bitcast(wt[0:1, :], jnp.uint32)
                dep = pltpu.bitcast(lax.shift_right_logical(lax.shift_right_logical(bits, jnp.uint32(16)),
                                                            jnp.uint32(16)), F32)
                w_tiles.append(wt.astype(BF16))
            w_rows.append(jnp.concatenate(w_tiles, axis=1))
        w = jnp.concatenate(w_rows, axis=0)
        acc_ref[...] += jnp.dot(vt_refs[sb][0], w, preferred_element_type=F32)

    @pl.when(j == nj - 1)
    def _():
        y_ref[...] = x1_ref[...] + acc_ref[...].T


def _peer_dense(h, u, vt, s1, s2, thr, stat, x1, tn, eb):
    n, d = x1.shape
    ne = u.shape[0]
    n_lt = tn // LANES
    sub_rows = _PEER_SUB_KEYS * PEER_NKEYS
    n_sub = eb // sub_rows
    assert vt.shape == (ne // sub_rows, d, sub_rows)
    keyed = pl.BlockSpec((n_lt, PEER_HEADS, PEER_NKEYS, LANES), lambda i, j: (i, 0, 0, 0))
    u_specs = [pl.BlockSpec((sub_rows, d), lambda i, j, k=k: (j * n_sub + k, 0)) for k in range(n_sub)]
    vt_specs = [pl.BlockSpec((1, d, sub_rows), lambda i, j, k=k: (j * n_sub + k, 0, 0)) for k in range(n_sub)]
    return pl.pallas_call(
        functools.partial(_peer_dense_kernel, n_sub=n_sub),
        grid=(n // tn, ne // eb),
        in_specs=[pl.BlockSpec((tn, d), lambda i, j: (i, 0))] + u_specs + vt_specs + [
                  keyed, keyed, keyed,
                  pl.BlockSpec((n_lt, PEER_HEADS, 8, LANES), lambda i, j: (i, 0, 0, 0)),
                  pl.BlockSpec((tn, d), lambda i, j: (i, 0))],
        out_specs=[pl.BlockSpec((tn, d), lambda i, j: (i, 0))],
        out_shape=[jax.ShapeDtypeStruct((n, d), F32)],
        scratch_shapes=[pltpu.VMEM((d, tn), F32),
                        pltpu.VMEM((n_lt, PEER_HEADS, PEER_NKEYS, LANES), F32),
                        pltpu.VMEM((n_lt, PEER_HEADS, PEER_NKEYS, LANES), F32)],
        compiler_params=_cparams(2),
        name="peer_dense",
    )(h, *([u] * n_sub), *([vt] * n_sub), s1, s2, thr, stat, x1)[0]


def _rope_tables(pos):
    half = HEAD_DIM // 2
    inv_freq = ROPE_THETA ** (-jnp.arange(half, dtype=F32) / half)
    ang = pos.astype(F32)[:, None] * inv_freq[None, :]
    cos = jnp.tile(jnp.cos(ang), (1, 2 * N_HEADS))
    sin = jnp.sin(ang)
    sin_signed = jnp.tile(jnp.concatenate([-sin, sin], axis=1), (1, N_HEADS))
    return cos, sin_signed


def _row_tile(n, pref):
    return pref if n % pref == 0 else n


def _seq_group(b, pref):
    return math.gcd(b, pref)


def _layer_params(l, norm_mix, w_in, q_norm, k_norm, conf_dw, conf_dw_b, conf_ln_g, conf_ln_b,
                  rwkv_mu, rwkv_w0, rwkv_w2, rwkv_a0, rwkv_a2, rwkv_g2, rwkv_kk, rwkv_ka, rwkv_rk,
                  rwkv_ln_g, rwkv_ln_b, sc_w, w_out, norm_ffn, peer_wq, peer_k1, peer_k2, peer_u, peer_v):
    row = lambda a: a[l].reshape(1, -1)
    att_c = 3 * GROUP
    conf_c = 2 * GROUP
    rw_c = rwkv_mu.shape[1]
    wi = w_in[l].astype(BF16)
    bounds = [0, att_c, att_c + conf_c, att_c + conf_c + rw_c, wi.shape[1]]
    rd, ra, rg = rwkv_w2.shape[1], rwkv_a2.shape[1], rwkv_g2.shape[1]
    zeros = lambda r: jnp.zeros((r, GROUP), F32)
    half = peer_k1.shape[2]
    return dict(
        norm_mix=row(norm_mix),
        w_in=[wi[:, bounds[i]:bounds[i + 1]] for i in range(4)],
        qg=jnp.tile(row(q_norm), (1, N_HEADS)), kg=jnp.tile(row(k_norm), (1, N_HEADS)),
        dw=conf_dw[l], dwb=row(conf_dw_b), clng=row(conf_ln_g), clnb=row(conf_ln_b), scw=sc_w[l],
        rwkv=dict(mu=row(rwkv_mu), w0=row(rwkv_w0), a0=row(rwkv_a0), kkw=row(rwkv_kk), ka=row(rwkv_ka),
                  rk=row(rwkv_rk), lng=row(rwkv_ln_g), lnb=row(rwkv_ln_b),
                  w2=jnp.concatenate([rwkv_w2[l], zeros(ra + rg)], axis=0),
                  a2=jnp.concatenate([zeros(rd), rwkv_a2[l], zeros(rg)], axis=0),
                  g2=jnp.concatenate([zeros(rd + ra), rwkv_g2[l]], axis=0)),
        w_out=w_out[l].astype(BF16).reshape(4, GROUP, -1),
        norm_ffn=row(norm_ffn), wq=peer_wq[l].astype(BF16),
        k1p=jnp.concatenate([peer_k1[l], jnp.zeros_like(peer_k1[l])], axis=1).astype(BF16),
        k2p=jnp.concatenate([jnp.zeros_like(peer_k2[l]), peer_k2[l]], axis=1).astype(BF16),
        u=peer_u[l].astype(BF16),
        vt=jnp.swapaxes(peer_v[l].astype(BF16).reshape(-1, _PEER_SUB_KEYS * PEER_NKEYS, peer_v.shape[2]), 1, 2),
    )


def _mix_and_ffn(x, oa, ob, oc, od, p):
    b, t, d = x.shape
    n = b * t
    flat = lambda a: a.reshape(n, a.shape[-1])
    tm = _row_tile(n, 512)
    x1, h2, q = _outproj(flat(oa), flat(ob), flat(oc), flat(od), flat(x), p["w_out"], p["norm_ffn"],
                         p["wq"], tm)
    tn = _row_tile(n, 512)
    s1, s2, thr, stat = _peer_select(q, p["k1p"], p["k2p"], tn)
    y = _peer_dense(h2, p["u"], p["vt"], s1, s2, thr, stat, x1, tn, PEER_EXPERT_BLOCK)
    return y.reshape(b, t, d)


def _common_mixers(x, p, conf_buf, shift_prev, wkv_prev, sc_buf, chunk, nseq, conv_tile):
    b, t, d = x.shape
    n = b * t
    att, conf, rw, sc = _proj(x.reshape(n, d), p["norm_mix"], p["w_in"], _row_tile(n, 512))
    att = att.reshape(b, t, -1)
    conf = conf.reshape(b, t, -1)
    rw = rw.reshape(b, t, -1)
    sc = sc.reshape(b, t, -1)
    ob, od, conf_new, sc_new = _conv(conf, sc, conf_buf, sc_buf, p["dw"], p["dwb"], p["clng"], p["clnb"],
                                     p["scw"], conv_tile)
    oc, st_out = _rwkv(rw, shift_prev[:, None, :], _state_to_blockdiag(wkv_prev), p["rwkv"], chunk, nseq)
    return att, ob, oc, od, conf_new, rw[:, -1, :], _blockdiag_to_state(st_out), sc_new


def kernel(x_prompt, x_sample, cache_k, cache_v, page_table, state_conf_conv, state_rwkv_shift, state_rwkv_wkv, state_short_conv, norm_mix, w_in, q_norm, k_norm, conf_dw, conf_dw_b, conf_ln_g, conf_ln_b, rwkv_mu, rwkv_w0, rwkv_w2, rwkv_a0, rwkv_a2, rwkv_g2, rwkv_kk, rwkv_ka, rwkv_rk, rwkv_ln_g, rwkv_ln_b, sc_w, w_out, norm_ffn, peer_wq, peer_k1, peer_k2, peer_u, peer_v):
    depth = w_in.shape[0]
    bp, seq, _ = x_prompt.shape
    db, dec_seq, _ = x_sample.shape
    n_pool = cache_k.shape[1]
    past_len = page_table.shape[1] * PAGE
    ck = jnp.transpose(cache_k, (0, 1, 3, 4, 2))
    cv = jnp.transpose(cache_v, (0, 1, 3, 4, 2))
    cos_p, sin_p = _rope_tables(jnp.arange(seq))
    cos_s, sin_s = _rope_tables(past_len + jnp.arange(dec_seq))
    dt = x_prompt.dtype

    yp, ys = x_prompt, x_sample
    outs_p = [[] for _ in range(6)]
    outs_s = [[] for _ in range(6)]
    for l in range(depth):
        p = _layer_params(l, norm_mix, w_in, q_norm, k_norm, conf_dw, conf_dw_b, conf_ln_g, conf_ln_b,
                          rwkv_mu, rwkv_w0, rwkv_w2, rwkv_a0, rwkv_a2, rwkv_g2, rwkv_kk, rwkv_ka,
                          rwkv_rk, rwkv_ln_g, rwkv_ln_b, sc_w, w_out, norm_ffn, peer_wq, peer_k1,
                          peer_k2, peer_u, peer_v)
        att, ob, oc, od, conf_new, shift_new, wkv_new, sc_new = _common_mixers(
            yp, p,
            jnp.zeros((bp, CONF_TAPS - 1, GROUP), dt),
            jnp.zeros((bp, rwkv_mu.shape[1]), dt),
            jnp.zeros((bp, N_HEADS, HEAD_DIM, HEAD_DIM), dt),
            jnp.zeros((bp, SC_TAPS - 1, GROUP), dt),
            chunk=64, nseq=_seq_group(bp, 8), conv_tile=_row_tile(seq, 512))
        oa, k_rot = _moba_prompt(att, cos_p, sin_p, p["qg"], p["kg"])
        new_p = (k_rot.reshape(bp, seq, N_HEADS, HEAD_DIM),
                 att[:, :, 2 * GROUP:].reshape(bp, seq, N_HEADS, HEAD_DIM),
                 conf_new, shift_new, wkv_new, sc_new)
        yp = _mix_and_ffn(yp, oa, ob, oc, od, p)

        att, ob, oc, od, conf_new, shift_new, wkv_new, sc_new = _common_mixers(
            ys, p, state_conf_conv[l], state_rwkv_shift[l], state_rwkv_wkv[l], state_short_conv[l],
            chunk=dec_seq, nseq=_seq_group(db, 8), conv_tile=dec_seq)
        sel, q_rot, k_rot = _decode_sel(att, cos_s, sin_s, p["qg"], p["kg"], ck, page_table, l)
        v_new = att[:, :, 2 * GROUP:]
        oa = _decode_att(sel, q_rot, k_rot, v_new, ck, cv, page_table, l)
        new_s = (k_rot.reshape(db, dec_seq, N_HEADS, HEAD_DIM),
                 v_new.reshape(db, dec_seq, N_HEADS, HEAD_DIM),
                 conf_new, shift_new, wkv_new, sc_new)
        ys = _mix_and_ffn(ys, oa, ob, oc, od, p)

        for lst, a in zip(outs_p, new_p):
            lst.append(a)
        for lst, a in zip(outs_s, new_s):
            lst.append(a)
    pk, pv, pconf, pshift, pwkv, psc = [jnp.stack(a) for a in outs_p]
    sk, sv, sconf, sshift, swkv, ssc = [jnp.stack(a) for a in outs_s]
    return (yp, ys, pk, pv, sk, sv, pconf, sconf, pshift, sshift, pwkv, swkv, psc, ssc)
```
